```python
import math
import jax, jax.numpy as jnp
from jax import lax
import numpy as np

D_MODEL = 2048
BATCH = 4
SEQ = 4096
DEPTH = 2

CTX_LEN = 256
GRID_W = 64
EPS = 1e-6

MIXER_KINDS = ('gdn', 'chunk_mlp')
N_MIXERS = len(MIXER_KINDS)

GDN_QK_HEADS = 16
GDN_V_HEADS = 32
GDN_HEAD_DIM = 128
GDN_KEY_DIM = GDN_QK_HEADS * GDN_HEAD_DIM
GDN_VAL_DIM = GDN_V_HEADS * GDN_HEAD_DIM
GDN_CONV = 5
GDN_CHUNK = 64
GDN_IN = 2 * GDN_KEY_DIM + 2 * GDN_VAL_DIM + 4 * GDN_V_HEADS

CM_WIDTH = 2 * D_MODEL
CM_GROUPS = 16
CM_GROUP_DIM = CM_WIDTH // CM_GROUPS
CM_CHUNK = 128
ROWS_PER_CHUNK = CM_CHUNK // GRID_W

FFN_DIM = 7 * D_MODEL // 2
N_EXPERTS = 8
TOP_K = 2

N_GDN = (DEPTH + 1) // 2
N_CM = DEPTH // 2

kernel_name = 'hybrid_gdn_chunkmlp_moe_dit'


def rmsnorm(x, g):
    xf = x.astype(jnp.float32)
    y = xf * lax.rsqrt(jnp.mean(xf * xf, axis=-1, keepdims=True) + EPS)
    return (y * g.astype(jnp.float32)).astype(x.dtype)


def layernorm(x, g, b):
    xf = x.astype(jnp.float32)
    mu = jnp.mean(xf, axis=-1, keepdims=True)
    var = jnp.mean(jnp.square(xf - mu), axis=-1, keepdims=True)
    return ((xf - mu) * lax.rsqrt(var + EPS) * g.astype(jnp.float32) + b.astype(jnp.float32)).astype(x.dtype)


def l2norm(t):
    tf = t.astype(jnp.float32)
    return tf * lax.rsqrt(jnp.sum(tf * tf, axis=-1, keepdims=True) + EPS)


def adaln(cond, w, b):
    m = jax.nn.silu(cond) @ w + b
    return jnp.split(m, 6, axis=-1)


def modulate(h, shift, scale):
    return h * (1.0 + scale) + shift


def centred_dwconv(x, w):
    k, ch = w.shape
    return lax.conv_general_dilated(
        x, w.reshape(k, 1, ch).astype(x.dtype), window_strides=(1,), padding=[(k // 2, k // 2)],
        dimension_numbers=('NWC', 'WIO', 'NWC'), feature_group_count=ch)


def gdn_chunked(q, k, v, g, beta, s0, with_output):
    f32 = jnp.float32
    nb, nh, length, dk = q.shape
    dv = v.shape[-1]
    n = length // GDN_CHUNK
    chunk = lambda t: t.astype(f32).reshape((nb, nh, n, GDN_CHUNK) + t.shape[3:])
    q, k, v, g, beta = chunk(q), chunk(k), chunk(v), chunk(g), chunk(beta)
    gcs = jnp.cumsum(g, axis=-1)
    incl = jnp.tril(jnp.ones((GDN_CHUNK, GDN_CHUNK), bool))
    strict = jnp.tril(jnp.ones((GDN_CHUNK, GDN_CHUNK), bool), -1)
    decay = jnp.exp(jnp.where(incl, gcs[..., :, None] - gcs[..., None, :], -jnp.inf))
    k_beta = k * beta[..., None]
    a_mat = jnp.where(strict, jnp.einsum('bhnid,bhnjd->bhnij', k_beta, k) * decay, 0.0)
    rhs = jnp.concatenate([v * beta[..., None], k_beta * jnp.exp(gcs)[..., None]], axis=-1)
    sol = lax.linalg.triangular_solve(a_mat, rhs, left_side=True, lower=True, unit_diagonal=True)
    u, w = sol[..., :dv], sol[..., dv:]
    g_last = gcs[..., -1]
    k_tail = k * jnp.exp(g_last[..., None] - gcs)[..., None]
    xs = [u, w, k_tail, g_last]
    if with_output:
        xs += [q * jnp.exp(gcs)[..., None], jnp.einsum('bhnid,bhnjd->bhnij', q, k) * decay]
    xs = tuple(jnp.moveaxis(t, 2, 0) for t in xs)

    def step(state, inp):
        u_n, w_n, kt_n, gl_n = inp[:4]
        v_new = u_n - jnp.einsum('bhck,bhkv->bhcv', w_n, state)
        s_next = state * jnp.exp(gl_n)[..., None, None] + jnp.einsum('bhck,bhcv->bhkv', kt_n, v_new)
        if not with_output:
            return s_next, None
        qh_n, qk_n = inp[4:]
        o_n = jnp.einsum('bhck,bhkv->bhcv', qh_n, state) + jnp.einsum('bhij,bhjv->bhiv', qk_n, v_new)
        return s_next, o_n

    s_fin, o = lax.scan(step, s0.astype(f32), xs)
    if with_output:
        o = jnp.moveaxis(o, 0, 2).reshape(nb, nh, length, dv)
    return s_fin, o


def gdn_project(u, w_in, conv_w, a_log, dt_bias):
    nb, length, _ = u.shape
    proj = u @ w_in
    qkv, z, b, a = jnp.split(proj, [2 * GDN_KEY_DIM + GDN_VAL_DIM, 2 * GDN_KEY_DIM + 2 * GDN_VAL_DIM,
                                    2 * GDN_KEY_DIM + 2 * GDN_VAL_DIM + 2 * GDN_V_HEADS], axis=-1)
    qkv = jax.nn.silu(centred_dwconv(qkv, conv_w))
    q, k, v = jnp.split(qkv, [GDN_KEY_DIM, 2 * GDN_KEY_DIM], axis=-1)
    heads = lambda t, h: jnp.transpose(t.reshape(nb, length, h, GDN_HEAD_DIM), (0, 2, 1, 3))
    rep = GDN_V_HEADS // GDN_QK_HEADS
    q = jnp.repeat(l2norm(heads(q, GDN_QK_HEADS)) * (GDN_HEAD_DIM ** -0.5), rep, axis=1)
    k = jnp.repeat(l2norm(heads(k, GDN_QK_HEADS)), rep, axis=1)
    v = heads(v, GDN_V_HEADS)
    b = b.reshape(nb, length, 2, GDN_V_HEADS).astype(jnp.float32)
    a = a.reshape(nb, length, 2, GDN_V_HEADS).astype(jnp.float32)
    beta = jax.nn.sigmoid(b)
    g = -jnp.exp(a_log.astype(jnp.float32)) * jax.nn.softplus(a + dt_bias.astype(jnp.float32))
    to_dir = lambda t: jnp.transpose(t, (2, 0, 3, 1))
    return q, k, v, z, to_dir(g), to_dir(beta)


def gdn_out(o, z, o_norm, w_out):
    nb, nh, length, dv = o.shape
    o = jnp.transpose(o, (0, 2, 1, 3))
    o = rmsnorm(o, o_norm) * jax.nn.silu(z.reshape(nb, length, nh, dv).astype(jnp.float32))
    return o.reshape(nb, length, nh * dv).astype(z.dtype) @ w_out


def gdn_mixer(u_lat, u_ctx, w_in, conv_w, a_log, dt_bias, o_norm, w_out, ctx_out):
    ql, kl, vl, zl, gl, bl = gdn_project(u_lat, w_in, conv_w, a_log, dt_bias)
    qc, kc, vc, zc, gc, bc = gdn_project(u_ctx, w_in, conv_w, a_log, dt_bias)
    s0 = jnp.zeros((u_lat.shape[0], GDN_V_HEADS, GDN_HEAD_DIM, GDN_HEAD_DIM), jnp.float32)
    o_lat, o_ctx = 0.0, 0.0
    for d in range(2):
        rev = (lambda t: jnp.flip(t, axis=2)) if d == 1 else (lambda t: t)
        s_ctx, oc = gdn_chunked(rev(qc), rev(kc), rev(vc), rev(gc[d]), rev(bc[d]), s0, ctx_out)
        _, ol = gdn_chunked(rev(ql), rev(kl), rev(vl), rev(gl[d]), rev(bl[d]), s_ctx, True)
        o_lat = o_lat + rev(ol)
        if ctx_out:
            o_ctx = o_ctx + rev(oc)
    y_lat = gdn_out(o_lat, zl, o_norm, w_out)
    y_ctx = gdn_out(o_ctx, zc, o_norm, w_out) if ctx_out else None
    return y_lat, y_ctx


def chunk_mlp(u, n_chunks, w_in, ln_g, ln_b, w_s, b_s, w_out):
    nb, length, _ = u.shape
    gu, gv = jnp.split(jax.nn.gelu(u @ w_in, approximate=False), 2, axis=-1)
    gv = layernorm(gv, ln_g, ln_b).reshape(nb, n_chunks, CM_CHUNK, CM_GROUPS, CM_GROUP_DIM)
    mixed = jnp.einsum('gpq,bnqgc->bnpgc', w_s, gv) + jnp.transpose(b_s)[:, :, None]
    return (gu * mixed.reshape(nb, length, CM_WIDTH)) @ w_out


def swiglu(t, w_gu, w_down):
    gate, up = jnp.split(t @ w_gu, 2, axis=-1)
    return (jax.nn.silu(gate) * up) @ w_down


def moe_swiglu(h, router_w, router_b, w_gu, w_down):
    nb, length, d = h.shape
    t = h.reshape(nb * length, d)
    logits = (t @ router_w).astype(jnp.float32) + router_b.astype(jnp.float32)
    top_val, top_idx = lax.top_k(logits, TOP_K)
    top_w = jax.nn.softmax(top_val, axis=-1)
    combine = jnp.sum(jax.nn.one_hot(top_idx, N_EXPERTS, dtype=jnp.float32) * top_w[..., None], axis=1)
    out = jnp.zeros_like(t)
    for e in range(N_EXPERTS):
        out = out + combine[:, e:e + 1].astype(t.dtype) * swiglu(t, w_gu[e], w_down[e])
    return out.reshape(nb, length, d)


def setup_inputs(seed: int = 0) -> dict:
    key = jax.random.key(seed)
    ks = iter(jax.random.split(key, 40))
    f32 = jnp.float32
    def dense(shape, fan_in):
        return jax.random.normal(next(ks), shape, f32) * fan_in ** -0.5
    def gain(shape):
        return 1.0 + 0.05 * jax.random.normal(next(ks), shape, f32)
    def small(shape, s):
        return s * jax.random.normal(next(ks), shape, f32)
    x = jax.random.normal(next(ks), (BATCH, SEQ, D_MODEL), f32)
    c = jax.random.normal(next(ks), (BATCH, D_MODEL), f32)
    ctx = jax.random.normal(next(ks), (BATCH, CTX_LEN, D_MODEL), f32)
    c_ctx = jax.random.normal(next(ks), (D_MODEL,), f32)
    dt = jnp.exp(jax.random.uniform(next(ks), (N_GDN, 2, GDN_V_HEADS), f32, math.log(1e-3), math.log(1e-1)))
    return {
        'x': x, 'c': c, 'ctx': ctx, 'c_ctx': c_ctx,
        'ada_w': dense((DEPTH, D_MODEL, 6 * D_MODEL), D_MODEL),
        'ada_b': small((DEPTH, 6 * D_MODEL), 0.02),
        'norm_mix': gain((DEPTH, D_MODEL)),
        'norm_ffn': gain((DEPTH, D_MODEL)),
        'norm_final': gain((D_MODEL,)),
        'gdn_w_in': dense((N_GDN, D_MODEL, GDN_IN), D_MODEL),
        'gdn_conv': dense((N_GDN, GDN_CONV, 2 * GDN_KEY_DIM + GDN_VAL_DIM), GDN_CONV),
        'gdn_a_log': jnp.log(jax.random.uniform(next(ks), (N_GDN, 2, GDN_V_HEADS), f32, 1.0, 16.0)),
        'gdn_dt_bias': dt + jnp.log(-jnp.expm1(-dt)),
        'gdn_o_norm': gain((N_GDN, GDN_HEAD_DIM)),
        'gdn_w_out': dense((N_GDN, GDN_VAL_DIM, D_MODEL), GDN_VAL_DIM),
        'cm_w_in': dense((N_CM, D_MODEL, 2 * CM_WIDTH), D_MODEL),
        'cm_ln_g': gain((N_CM, CM_WIDTH)),
        'cm_ln_b': small((N_CM, CM_WIDTH), 0.02),
        'cm_w_s': dense((N_CM, CM_GROUPS, CM_CHUNK, CM_CHUNK), CM_CHUNK) * 0.5,
        'cm_b_s': 1.0 + small((N_CM, CM_GROUPS, CM_CHUNK), 0.1),
        'cm_w_out': dense((N_CM, CM_WIDTH, D_MODEL), CM_WIDTH),
        'ffn_w_gu': dense((N_GDN, D_MODEL, 2 * FFN_DIM), D_MODEL),
        'ffn_w_down': dense((N_GDN, FFN_DIM, D_MODEL), FFN_DIM),
        'moe_router': dense((N_CM, D_MODEL, N_EXPERTS), D_MODEL),
        'moe_router_b': small((N_CM, N_EXPERTS), 0.01),
        'moe_w_gu': dense((N_CM, N_EXPERTS, D_MODEL, 2 * FFN_DIM), D_MODEL),
        'moe_w_down': dense((N_CM, N_EXPERTS, FFN_DIM, D_MODEL), FFN_DIM),
    }


def reference(x, c, ctx, c_ctx, ada_w, ada_b, norm_mix, norm_ffn, norm_final,
              gdn_w_in, gdn_conv, gdn_a_log, gdn_dt_bias, gdn_o_norm, gdn_w_out,
              cm_w_in, cm_ln_g, cm_ln_b, cm_w_s, cm_b_s, cm_w_out,
              ffn_w_gu, ffn_w_down, moe_router, moe_router_b, moe_w_gu, moe_w_down):
    length = x.shape[1]
    rows = length // GRID_W
    lat_chunks = rows // ROWS_PER_CHUNK
    cond_lat = c[:, None, :]
    cond_ctx = c_ctx[None, None, :]

    def ffn(i, j, t):
        if i % 2 == 0:
            return swiglu(t, ffn_w_gu[j], ffn_w_down[j])
        return moe_swiglu(t, moe_router[j], moe_router_b[j], moe_w_gu[j], moe_w_down[j])

    h, hc = x, ctx
    for i in range(DEPTH):
        j = i // N_MIXERS
        is_gdn = MIXER_KINDS[i % N_MIXERS] == 'gdn'
        ctx_later = any(MIXER_KINDS[l % N_MIXERS] == 'gdn' for l in range(i + 1, DEPTH))
        sh_m, sc_m, gt_m, sh_f, sc_f, gt_f = adaln(cond_lat, ada_w[i], ada_b[i])
        u = modulate(rmsnorm(h, norm_mix[i]), sh_m, sc_m)
        if is_gdn or ctx_later:
            csh_m, csc_m, cgt_m, csh_f, csc_f, cgt_f = adaln(cond_ctx, ada_w[i], ada_b[i])
            uc = modulate(rmsnorm(hc, norm_mix[i]), csh_m, csc_m)
        if is_gdn:
            y, yc = gdn_mixer(u, uc, gdn_w_in[j], gdn_conv[j], gdn_a_log[j], gdn_dt_bias[j],
                              gdn_o_norm[j], gdn_w_out[j], ctx_later)
        else:
            cm = (cm_w_in[j], cm_ln_g[j], cm_ln_b[j], cm_w_s[j], cm_b_s[j], cm_w_out[j])
            y = chunk_mlp(u, lat_chunks, *cm)
            yc = chunk_mlp(uc, hc.shape[1] // CM_CHUNK, *cm) if ctx_later else None
        h = h + gt_m * y
        h = h + gt_f * ffn(i, j, modulate(rmsnorm(h, norm_ffn[i]), sh_f, sc_f))
        if ctx_later:
            hc = hc + cgt_m * yc
            hc = hc + cgt_f * ffn(i, j, modulate(rmsnorm(hc, norm_ffn[i]), csh_f, csc_f))
    return rmsnorm(h, norm_final)
```

```python
import functools

import jax
import jax.numpy as jnp
from jax import lax
from jax.experimental import pallas as pl
from jax.experimental.pallas import tpu as pltpu

F32 = jnp.float32
BF16 = jnp.bfloat16
EPS = 1e-6
HIGHEST = lax.Precision.HIGHEST

LANES = 128
VMEM_LIMIT_BYTES = 56 * 1024 * 1024

GDN_HEAD_DIM = 128
GDN_CHUNK = 64
GDN_CONV = 5
CM_CHUNK = 128
CM_GROUPS = 16
N_EXPERTS = 8
TOP_K = 2

ROW_TILE = 256
GDN_STEP = 256
GDN_QK_HEADS_PER_STEP = 2
MM_TM = 1024
MOE_TM = 1024
GATHER_ROWS = 256


def _cparams(sem):
    return pltpu.CompilerParams(dimension_semantics=sem, vmem_limit_bytes=VMEM_LIMIT_BYTES)


def _tile(n, preferred):
    t = preferred
    while n % t:
        t //= 2
    return t


def _silu(x):
    return x * (1.0 / (1.0 + jnp.exp(-x)))


def _dot(a, b):
    return jnp.dot(a.astype(BF16), b.astype(BF16), preferred_element_type=F32)


def _dot_nt(a, b):
    return lax.dot_general(a.astype(BF16), b.astype(BF16), (((1,), (1,)), ((), ())),
                           preferred_element_type=F32)


def _ada_kernel(c_ref, w_ref, b_ref, o_ref):
    s = _silu(c_ref[...])
    o_ref[0] = jnp.dot(s, w_ref[0], preferred_element_type=F32, precision=HIGHEST) + b_ref[0]


def _ada(cond8, ada_w, ada_b, tn=1024):
    depth, d, n = ada_w.shape
    return pl.pallas_call(
        _ada_kernel,
        grid=(depth, n // tn),
        in_specs=[pl.BlockSpec((8, d), lambda l, j: (0, 0)),
                  pl.BlockSpec((1, d, tn), lambda l, j: (l, 0, j)),
                  pl.BlockSpec((1, 1, tn), lambda l, j: (l, 0, j))],
        out_specs=pl.BlockSpec((1, 8, tn), lambda l, j: (l, 0, j)),
        out_shape=jax.ShapeDtypeStruct((depth, 8, n), F32),
        compiler_params=_cparams(("parallel", "parallel")),
        name="ada",
    )(cond8, ada_w, ada_b.reshape(depth, 1, n))


def _norm_mod_kernel(h_ref, g_ref, m_ref, o_ref):
    x = h_ref[0]
    y = x * lax.rsqrt(jnp.mean(x * x, axis=-1, keepdims=True) + EPS) * g_ref[...]
    o_ref[0] = (y * (1.0 + m_ref[0, 0, 1:2, :]) + m_ref[0, 0, 0:1, :]).astype(o_ref.dtype)


def _norm_mod(h, gain, mods, n_ctx_tiles):
    b, s, d = h.shape
    return pl.pallas_call(
        _norm_mod_kernel,
        grid=(b, s // ROW_TILE),
        in_specs=[pl.BlockSpec((1, ROW_TILE, d), lambda i, r: (i, r, 0)),
                  pl.BlockSpec((1, d), lambda i, r: (0, 0)),
                  pl.BlockSpec((1, 1, 2, d), lambda i, r: (i, jnp.where(r < n_ctx_tiles, 0, 1), 0, 0))],
        out_specs=pl.BlockSpec((1, ROW_TILE, d), lambda i, r: (i, r, 0)),
        out_shape=jax.ShapeDtypeStruct((b, s, d), BF16),
        compiler_params=_cparams(("parallel", "parallel")),
        name="norm_mod",
    )(h, gain.reshape(1, d), mods)


def _norm_mod_router_kernel(h_ref, g_ref, m_ref, rw_ref, rb_ref, o_ref, idx_ref, wt_ref):
    x = h_ref[0]
    y = x * lax.rsqrt(jnp.mean(x * x, axis=-1, keepdims=True) + EPS) * g_ref[...]
    t = y * (1.0 + m_ref[0, 0, 1:2, :]) + m_ref[0, 0, 0:1, :]
    o_ref[0] = t.astype(o_ref.dtype)
    logits = jnp.dot(t, rw_ref[...], preferred_element_type=F32, precision=HIGHEST) + rb_ref[...]
    lane = lax.broadcasted_iota(jnp.int32, logits.shape, 1)
    m1 = jnp.max(logits, axis=-1, keepdims=True)
    i1 = jnp.min(jnp.where(logits == m1, lane, LANES), axis=-1, keepdims=True)
    rest = jnp.where(lane == i1, -jnp.inf, logits)
    m2 = jnp.max(rest, axis=-1, keepdims=True)
    i2 = jnp.min(jnp.where(rest == m2, lane, LANES), axis=-1, keepdims=True)
    e = jnp.exp(m2 - m1)
    w1 = 1.0 / (1.0 + e)
    w2 = e * w1
    idx_ref[0] = jnp.where(lane == 0, i1, jnp.where(lane == 1, i2, 0))
    wt_ref[0] = jnp.where(lane == 0, w1, jnp.where(lane == 1, w2, 0.0))


def _norm_mod_router(h, gain, mods, router_w, router_b):
    b, s, d = h.shape
    ne = router_w.shape[1]
    rw = jnp.zeros((d, LANES), F32).at[:, :ne].set(router_w)
    rb = jnp.full((1, LANES), -1e30, F32).at[0, :ne].set(router_b)
    return pl.pallas_call(
        _norm_mod_router_kernel,
        grid=(b, s // ROW_TILE),
        in_specs=[pl.BlockSpec((1, ROW_TILE, d), lambda i, r: (i, r, 0)),
                  pl.BlockSpec((1, d), lambda i, r: (0, 0)),
                  pl.BlockSpec((1, 1, 2, d), lambda i, r: (i, 1, 0, 0)),
                  pl.BlockSpec((d, LANES), lambda i, r: (0, 0)),
                  pl.BlockSpec((1, LANES), lambda i, r: (0, 0))],
        out_specs=[pl.BlockSpec((1, ROW_TILE, d), lambda i, r: (i, r, 0)),
                   pl.BlockSpec((1, ROW_TILE, LANES), lambda i, r: (i, r, 0)),
                   pl.BlockSpec((1, ROW_TILE, LANES), lambda i, r: (i, r, 0))],
        out_shape=[jax.ShapeDtypeStruct((b, s, d), BF16),
                   jax.ShapeDtypeStruct((b, s, LANES), jnp.int32),
                   jax.ShapeDtypeStruct((b, s, LANES), F32)],
        compiler_params=_cparams(("parallel", "parallel")),
        name="norm_mod_router",
    )(h, gain.reshape(1, d), mods, rw, rb)


def _gelu_exact(x):
    return 0.5 * x * (1.0 + lax.erf(x * (2.0 ** -0.5)))


def _mm_kernel(*refs, nk, mode):
    n_in = {"swiglu": 3, "resid": 4}.get(mode, 2)
    a_ref, w_ref = refs[:2]
    o_ref = refs[n_in]
    if mode == "swiglu":
        w2_ref = refs[2]
    if mode == "resid":
        h_ref, g_ref = refs[2:4]
    if nk > 1:
        acc_ref = refs[n_in + 1]
        if mode == "swiglu":
            acc2_ref = refs[n_in + 2]
    k = pl.program_id(2)
    a = a_ref[...]
    p = jnp.dot(a, w_ref[...].astype(BF16), preferred_element_type=F32)
    if mode == "swiglu":
        p2 = jnp.dot(a, w2_ref[...].astype(BF16), preferred_element_type=F32)

    if nk > 1:
        @pl.when(k == 0)
        def _():
            acc_ref[...] = p
            if mode == "swiglu":
                acc2_ref[...] = p2

        @pl.when(k > 0)
        def _():
            acc_ref[...] += p
            if mode == "swiglu":
                acc2_ref[...] += p2

    def finish():
        r = acc_ref[...] if nk > 1 else p
        if mode == "plain":
            o_ref[...] = r.astype(o_ref.dtype)
        elif mode == "gelu":
            o_ref[...] = _gelu_exact(r).astype(o_ref.dtype)
        elif mode == "swiglu":
            r2 = acc2_ref[...] if nk > 1 else p2
            o_ref[...] = (_silu(r) * r2).astype(o_ref.dtype)
        else:
            o_ref[...] = h_ref[...] + g_ref[0] * r

    if nk > 1:
        pl.when(k == nk - 1)(finish)
    else:
        finish()


def _mm(a, w, *, mode, tm, tn, tk, out_dtype, n_out=None, w_col0=0, w2_col0=0, h=None, gate=None,
        rows_per_gate=None):
    m, kdim = a.shape
    n_out = w.shape[1] if n_out is None else n_out
    nk = kdim // tk
    assert m % tm == 0 and n_out % tn == 0 and kdim % tk == 0 and w_col0 % tn == 0 and w2_col0 % tn == 0
    cj, cj2 = w_col0 // tn, w2_col0 // tn
    in_specs = [pl.BlockSpec((tm, tk), lambda i, j, k: (i, k)),
                pl.BlockSpec((tk, tn), lambda i, j, k: (k, j + cj))]
    args = [a, w]
    n_acc = 0 if nk == 1 else (2 if mode == "swiglu" else 1)
    scratch = [pltpu.VMEM((tm, tn), F32)] * n_acc
    if mode == "swiglu":
        in_specs.append(pl.BlockSpec((tk, tn), lambda i, j, k: (k, j + cj2)))
        args.append(w)
    if mode == "resid":
        tiles_per_gate = rows_per_gate // tm
        in_specs += [pl.BlockSpec((tm, tn), lambda i, j, k: (i, j)),
                     pl.BlockSpec((1, 1, tn), lambda i, j, k: (i // tiles_per_gate, 0, j))]
        args += [h, gate]
    return pl.pallas_call(
        functools.partial(_mm_kernel, nk=nk, mode=mode),
        grid=(m // tm, n_out // tn, nk),
        in_specs=in_specs,
        out_specs=pl.BlockSpec((tm, tn), lambda i, j, k: (i, j)),
        out_shape=jax.ShapeDtypeStruct((m, n_out), out_dtype),
        scratch_shapes=scratch,
        compiler_params=_cparams(("parallel", "parallel", "arbitrary")),
        name="mm_" + mode,
    )(*args)


def _conv_kernel(x_ref, w_ref, o_ref, *, n_ctx, n_q_blocks, n_k_blocks):
    x = x_ref[0].astype(F32)
    s = x.shape[0]
    row = lax.broadcasted_iota(jnp.int32, x.shape, 0)
    seg_lo = jnp.where(row < n_ctx, 0, n_ctx)
    seg_hi = jnp.where(row < n_ctx, n_ctx, s)
    half = GDN_CONV // 2
    y = x * w_ref[half:half + 1, :]
    for o in range(-half, half + 1):
        if o == 0:
            continue
        shifted = pltpu.roll(x, (-o) % s, 0)
        src = row + o
        ok = (src >= seg_lo) & (src < seg_hi)
        y = y + jnp.where(ok, shifted, 0.0) * w_ref[o + half:o + half + 1, :]
    y = _silu(y)
    cb = pl.program_id(1)
    fac = lax.rsqrt(jnp.sum(y * y, axis=-1, keepdims=True) + EPS)
    q_scale = GDN_HEAD_DIM ** -0.5
    scale = jnp.where(cb < n_q_blocks, fac * q_scale, jnp.where(cb < n_q_blocks + n_k_blocks, fac, 1.0))
    o_ref[0] = (y * scale).astype(o_ref.dtype)


def _gdn_conv(proj, conv_w, n_ctx, key_dim):
    b, s, _ = proj.shape
    c = conv_w.shape[1]
    nqb = key_dim // LANES
    return pl.pallas_call(
        functools.partial(_conv_kernel, n_ctx=n_ctx, n_q_blocks=nqb, n_k_blocks=nqb),
        grid=(b, c // LANES),
        in_specs=[pl.BlockSpec((1, s, LANES), lambda i, j: (i, 0, j)),
                  pl.BlockSpec((GDN_CONV, LANES), lambda i, j: (0, j))],
        out_specs=pl.BlockSpec((1, s, LANES), lambda i, j: (i, 0, j)),
        out_shape=jax.ShapeDtypeStruct((b, s, c), BF16),
        compiler_params=_cparams(("parallel", "parallel")),
        name="gdn_conv",
    )(proj, conv_w)


def _gates_kernel(ba_ref, alog_ref, dtb_ref, o_ref, *, n_dir_lanes):
    x = ba_ref[0]
    lc = lax.broadcasted_iota(jnp.int32, (GDN_CHUNK, LANES), 1)
    half = 2 * n_dir_lanes
    beta = 1.0 / (1.0 + jnp.exp(-x))
    z = x + dtb_ref[...]
    softplus = jnp.maximum(z, 0.0) + jnp.log1p(jnp.exp(-jnp.abs(z)))
    g = -jnp.exp(alog_ref[...]) * softplus
    ri = lax.broadcasted_iota(jnp.int32, (GDN_CHUNK, GDN_CHUNK), 0)
    ci = lax.broadcasted_iota(jnp.int32, (GDN_CHUNK, GDN_CHUNK), 1)
    lower = (ri >= ci).astype(F32)
    upper = (ri <= ci).astype(F32)
    for c in range(x.shape[0] // GDN_CHUNK):
        rows = slice(c * GDN_CHUNK, (c + 1) * GDN_CHUNK)
        gc = g[rows]
        fwd = jnp.dot(lower, gc, preferred_element_type=F32, precision=HIGHEST)
        bwd = jnp.dot(upper, gc, preferred_element_type=F32, precision=HIGHEST)
        gcs =jnp.where(lc < half + n_dir_lanes, fwd, bwd)
        o_ref[0, rows, :] = jnp.where(lc < half, beta[rows], gcs)


def _gdn_gates(ba, a_log, dt_bias):
    b, s, _ = ba.shape
    nd = a_log.shape[-1]
    zeros = jnp.zeros((2 * nd,), F32)
    alog = jnp.concatenate([zeros, a_log.reshape(-1)]).reshape(1, LANES)
    dtb = jnp.concatenate([zeros, dt_bias.reshape(-1)]).reshape(1, LANES)
    return pl.pallas_call(
        functools.partial(_gates_kernel, n_dir_lanes=nd),
        grid=(b, s // GDN_STEP),
        in_specs=[pl.BlockSpec((1, GDN_STEP, LANES), lambda i, r: (i, r, 0)),
                  pl.BlockSpec((1, LANES), lambda i, r: (0, 0)),
                  pl.BlockSpec((1, LANES), lambda i, r: (0, 0))],
        out_specs=pl.BlockSpec((1, GDN_STEP, LANES), lambda i, r: (i, r, 0)),
        out_shape=jax.ShapeDtypeStruct((b, s, LANES), F32),
        compiler_params=_cparams(("parallel", "parallel")),
        name="gdn_gates",
    )(ba, alog, dtb)


def _unit_lower_inverse(a, eye, blockdiag):
    a_d = jnp.where(blockdiag, a, 0.0)
    a_o = a - a_d
    p2 = _dot(a_d, a_d)
    p4 = _dot(p2, p2)
    p8 = _dot(p4, p4)
    t_d = _dot(_dot(_dot(eye - a_d, eye + p2), eye + p4), eye + p8)
    n = _dot(t_d, a_o)
    n2 = _dot(n, n)
    r = _dot(eye - n, eye + n2)
    return _dot(r, t_d)


def _gdn_kernel(q_ref, k_ref, v_ref, col_ref, o_ref, s_ref, *, direction, n_dir_lanes):
    hg = pl.program_id(1)
    step = pl.program_id(2)
    hq = GDN_QK_HEADS_PER_STEP
    nv = 2 * hq
    c = GDN_CHUNK
    dh = GDN_HEAD_DIM
    n_chunks = GDN_STEP // c

    @pl.when(step == 0)
    def _():
        s_ref[...] = jnp.zeros_like(s_ref)

    col = col_ref[0]
    lane0 = direction * n_dir_lanes + hg * nv
    beta_g = pltpu.roll(col, (LANES - lane0) % LANES, 1)
    gcs_g = pltpu.roll(col, (LANES // 2 - lane0) % LANES, 1)
    beta_r = beta_g.T
    gcs_r = gcs_g.T

    ri = lax.broadcasted_iota(jnp.int32, (c, c), 0)
    ci = lax.broadcasted_iota(jnp.int32, (c, c), 1)
    if direction == 0:
        incl, strict = ri >= ci, ri > ci
    else:
        incl, strict = ri <= ci, ri < ci
    eye = (ri == ci).astype(F32)
    blockdiag = (ri // 16) == (ci // 16)
    eye_d = (lax.broadcasted_iota(jnp.int32, (dh, dh), 0) == lax.broadcasted_iota(jnp.int32, (dh, dh), 1)).astype(BF16)

    order = range(n_chunks) if direction == 0 else range(n_chunks - 1, -1, -1)
    for p in order:
        rows = slice(p * c, (p + 1) * c)
        for qh in range(hq):
            ql = slice(qh * dh, (qh + 1) * dh)
            k = k_ref[0, rows, ql]
            q = q_ref[0, rows, ql]
            kk = _dot_nt(k, k)
            qk = _dot_nt(q, k)
            k_t = _dot_nt(eye_d, k)
            kf = k.astype(F32)
            qf = q.astype(F32)
            for jj in range(2):
                j = 2 * qh + jj
                v = v_ref[0, rows, j * dh:(j + 1) * dh]
                bcol = beta_g[rows, j:j + 1]
                gcol = gcs_g[rows, j:j + 1]
                brow = beta_r[j:j + 1, rows]
                grow = gcs_r[j:j + 1, rows]
                glast = grow[:, c - 1:c] if direction == 0 else grow[:, 0:1]
                decay = jnp.exp(jnp.where(incl, gcol - grow, -jnp.inf))
                a = jnp.where(strict, kk * bcol * decay, 0.0)
                t = _unit_lower_inverse(a, eye, blockdiag)
                tb = t * brow
                u = _dot(tb, v)
                w = _dot(tb * jnp.exp(grow), k)
                state = s_ref[j]
                v_new = u - _dot(w, state)
                o = _dot(qf * jnp.exp(gcol), state) + _dot(jnp.where(incl, qk * decay, 0.0), v_new)
                kt_tail = k_t * jnp.exp(glast - grow)
                s_ref[j] = state * jnp.exp(glast) + _dot(kt_tail, v_new)
                o_ref[0, rows, j * dh:(j + 1) * dh] = o.astype(o_ref.dtype)


def _gdn_scan(qkv, col, direction, n_ctx, key_dim, n_v_heads):
    b, s, _ = qkv.shape
    hq = GDN_QK_HEADS_PER_STEP
    nv = 2 * hq
    n_steps = s // GDN_STEP
    n_ctx_steps = n_ctx // GDN_STEP
    assert n_ctx_steps == 1
    qw = hq * GDN_HEAD_DIM
    vw = nv * GDN_HEAD_DIM
    n_groups = key_dim // qw
    k_blk0 = key_dim // qw
    v_blk0 = 2 * key_dim // vw

    def blk(st):
        if direction == 0:
            return st
        return jnp.where(st == 0, 0, n_steps - st)

    def oblk(st):
        return blk(jnp.maximum(st, 1)) - 1

    return pl.pallas_call(
        functools.partial(_gdn_kernel, direction=direction, n_dir_lanes=n_v_heads),
        grid=(b, n_groups, n_steps),
        in_specs=[pl.BlockSpec((1, GDN_STEP, qw), lambda i, g, st: (i, blk(st), g)),
                  pl.BlockSpec((1, GDN_STEP, qw), lambda i, g, st: (i, blk(st), k_blk0 + g)),
                  pl.BlockSpec((1, GDN_STEP, vw), lambda i, g, st: (i, blk(st), v_blk0 + g)),
                  pl.BlockSpec((1, GDN_STEP, LANES), lambda i, g, st: (i, blk(st), 0))],
        out_specs=pl.BlockSpec((1, GDN_STEP, vw), lambda i, g, st: (i, oblk(st), g)),
        out_shape=jax.ShapeDtypeStruct((b, s - n_ctx, n_v_heads * GDN_HEAD_DIM), BF16),
        scratch_shapes=[pltpu.VMEM((nv, GDN_HEAD_DIM, GDN_HEAD_DIM), F32)],
        compiler_params=_cparams(("parallel", "parallel", "arbitrary")),
        name="gdn_scan_d%d" % direction,
    )(qkv, qkv, qkv, col)


def _gdn_out_kernel(o0_ref, o1_ref, z_ref, g_ref, y_ref):
    dh = GDN_HEAD_DIM
    for hd in range(o0_ref.shape[2] // dh):
        sl = slice(hd * dh, (hd + 1) * dh)
        o = o0_ref[0, :, sl].astype(F32) + o1_ref[0, :, sl].astype(F32)
        y = o * lax.rsqrt(jnp.mean(o * o, axis=-1, keepdims=True) + EPS) * g_ref[...]
        y_ref[0, :, sl] = (y * _silu(z_ref[0, :, sl].astype(F32))).astype(y_ref.dtype)


def _gdn_out_norm(o0, o1, proj, o_norm, n_ctx, z_col0):
    b, l, w = o0.shape
    ctx_tiles = n_ctx // ROW_TILE
    zb = z_col0 // w
    return pl.pallas_call(
        _gdn_out_kernel,
        grid=(b, l // ROW_TILE),
        in_specs=[pl.BlockSpec((1, ROW_TILE, w), lambda i, r: (i, r, 0)),
                  pl.BlockSpec((1, ROW_TILE, w), lambda i, r: (i, r, 0)),
                  pl.BlockSpec((1, ROW_TILE, w), lambda i, r: (i, r + ctx_tiles, zb)),
                  pl.BlockSpec((1, GDN_HEAD_DIM), lambda i, r: (0, 0))],
        out_specs=pl.BlockSpec((1, ROW_TILE, w), lambda i, r: (i, r, 0)),
        out_shape=jax.ShapeDtypeStruct((b, l, w), BF16),
        compiler_params=_cparams(("parallel", "parallel")),
        name="gdn_out_norm",
    )(o0, o1, proj, o_norm.reshape(1, -1))


def _cm_kernel(gu_ref, gv_ref, lg_ref, lb_ref, ws_ref, bs_ref, o_ref):
    gd = gu_ref.shape[1] // CM_GROUPS
    for ch in range(gu_ref.shape[0] // CM_CHUNK):
        rows = slice(ch * CM_CHUNK, (ch + 1) * CM_CHUNK)
        gv = gv_ref[rows, :].astype(F32)
        mu = jnp.mean(gv, axis=-1, keepdims=True)
        xc = gv - mu
        var = jnp.mean(xc * xc, axis=-1, keepdims=True)
        gvn = xc * lax.rsqrt(var + EPS) * lg_ref[...] + lb_ref[...]
        for g in range(CM_GROUPS):
            sl = slice(g * gd, (g + 1) * gd)
            mixed = _dot(ws_ref[g], gvn[:, sl]) + bs_ref[:, g:g + 1]
            o_ref[rows, sl] = (gu_ref[rows, sl].astype(F32) * mixed).astype(o_ref.dtype)


def _cm_spatial(guv, ln_g, ln_b, w_s, b_s):
    t, w2 = guv.shape
    w = w2 // 2
    rt = 2 * CM_CHUNK
    return pl.pallas_call(
        _cm_kernel,
        grid=(t // rt,),
        in_specs=[pl.BlockSpec((rt, w), lambda i: (i, 0)),
                  pl.BlockSpec((rt, w), lambda i: (i, 1)),
                  pl.BlockSpec((1, w), lambda i: (0, 0)),
                  pl.BlockSpec((1, w), lambda i: (0, 0)),
                  pl.BlockSpec((CM_GROUPS, CM_CHUNK, CM_CHUNK), lambda i: (0, 0, 0)),
                  pl.BlockSpec((CM_CHUNK, CM_GROUPS), lambda i: (0, 0))],
        out_specs=pl.BlockSpec((rt, w), lambda i: (i, 0)),
        out_shape=jax.ShapeDtypeStruct((t, w), BF16),
        compiler_params=_cparams(("parallel",)),
        name="cm_spatial",
    )(guv, guv, ln_g.reshape(1, w), ln_b.reshape(1, w), w_s, jnp.transpose(b_s))


def _moe_plan(idx2, wt2):
    t = idx2.shape[0]
    a = t * TOP_K
    e = idx2.reshape(a)
    onehot = (e[:, None] == jnp.arange(N_EXPERTS, dtype=jnp.int32)[None, :]).astype(jnp.int32)
    rank = jnp.sum((jnp.cumsum(onehot, axis=0) - onehot) * onehot, axis=1)
    counts = jnp.sum(onehot, axis=0)
    padded = ((counts + MOE_TM - 1) // MOE_TM) * MOE_TM
    ends = jnp.cumsum(padded)
    starts = ends - padded
    pos = starts[e] + rank
    a_pad = a + N_EXPERTS * MOE_TM
    row_token = jnp.zeros((a_pad,), jnp.int32).at[pos].set(jnp.arange(a, dtype=jnp.int32) // TOP_K)
    row_w = jnp.zeros((a_pad,), F32).at[pos].set(wt2.reshape(a))
    n_tiles = a_pad // MOE_TM
    tile_start = jnp.arange(n_tiles, dtype=jnp.int32) * MOE_TM
    tile_expert = jnp.minimum(jnp.searchsorted(ends, tile_start, side="right"), N_EXPERTS - 1).astype(jnp.int32)
    n_used = (ends[-1] // MOE_TM).astype(jnp.int32).reshape(1)
    return pos.reshape(t, TOP_K), row_token, row_w.reshape(a_pad, 1), tile_expert, n_used


def _gather_kernel(idx_ref, src_ref, dst_ref, sem):
    base = pl.program_id(0) * GATHER_ROWS

    def issue(r, carry):
        pltpu.make_async_copy(src_ref.at[idx_ref[0, 0, r]], dst_ref.at[base + r], sem).start()
        return carry

    lax.fori_loop(0, GATHER_ROWS, issue, 0)

    def drain(r, carry):
        pltpu.make_async_copy(src_ref.at[0], dst_ref.at[base], sem).wait()
        return carry

    lax.fori_loop(0, GATHER_ROWS, drain, 0)


def _gather_rows(src3, row_idx):
    r = row_idx.shape[0]
    nb = r // GATHER_ROWS
    return pl.pallas_call(
        _gather_kernel,
        grid=(nb,),
        in_specs=[pl.BlockSpec((1, 1, GATHER_ROWS), lambda i: (i, 0, 0), memory_space=pltpu.SMEM),
                  pl.BlockSpec(memory_space=pl.ANY)],
        out_specs=pl.BlockSpec(memory_space=pl.ANY),
        out_shape=jax.ShapeDtypeStruct((r,) + src3.shape[1:], src3.dtype),
        scratch_shapes=[pltpu.SemaphoreType.DMA(())],
        compiler_params=_cparams(("arbitrary",)),
        name="gather_rows",
    )(row_idx.reshape(nb, 1, GATHER_ROWS), src3)


def _moe_gu_kernel(te_ref, nu_ref, x_ref, wg_ref, wu_ref, o_ref):
    used = pl.program_id(0) < nu_ref[0]

    @pl.when(used)
    def _():
        x = x_ref[...]
        g = jnp.dot(x, wg_ref[0].astype(BF16), preferred_element_type=F32)
        u = jnp.dot(x, wu_ref[0].astype(BF16), preferred_element_type=F32)
        o_ref[...] = (_silu(g) * u).astype(o_ref.dtype)

    @pl.when(jnp.logical_not(used))
    def _():
        o_ref[...] = jnp.zeros_like(o_ref)


def _moe_gu(xs, w_gu, tile_expert, n_used, tn=512):
    a_pad, d = xs.shape
    f = w_gu.shape[2] // 2
    nj = f // tn
    return pl.pallas_call(
        _moe_gu_kernel,
        grid_spec=pltpu.PrefetchScalarGridSpec(
            num_scalar_prefetch=2,
            grid=(a_pad // MOE_TM, nj),
            in_specs=[pl.BlockSpec((MOE_TM, d), lambda i, j, te, nu: (i, 0)),
                      pl.BlockSpec((1, d, tn), lambda i, j, te, nu: (te[i], 0, j)),
                      pl.BlockSpec((1, d, tn), lambda i, j, te, nu: (te[i], 0, j + nj))],
            out_specs=pl.BlockSpec((MOE_TM, tn), lambda i, j, te, nu: (i, j))),
        out_shape=jax.ShapeDtypeStruct((a_pad, f), BF16),
        compiler_params=_cparams(("parallel", "parallel")),
        name="moe_gu",
    )(tile_expert, n_used, xs, w_gu, w_gu)


def _moe_down_kernel(te_ref, nu_ref, a_ref, w_ref, rw_ref, o_ref, acc_ref, *, nk):
    k = pl.program_id(2)
    used = pl.program_id(0) < nu_ref[0]

    @pl.when(jnp.logical_not(used) & (k == nk - 1))
    def _():
        o_ref[...] = jnp.zeros_like(o_ref)

    @pl.when(used)
    def _():
        p = jnp.dot(a_ref[...], w_ref[0].astype(BF16), preferred_element_type=F32)

        @pl.when(k == 0)
        def _():
            acc_ref[...] = p

        @pl.when(k > 0)
        def _():
            acc_ref[...] += p

        @pl.when(k == nk - 1)
        def _():
            o_ref[...] = (acc_ref[...] * rw_ref[...]).astype(o_ref.dtype)


def _moe_down(hmid, w_down, row_w, tile_expert, n_used, tn=1024, tk=1792):
    a_pad, f = hmid.shape
    d = w_down.shape[2]
    nk = f // tk
    return pl.pallas_call(
        functools.partial(_moe_down_kernel, nk=nk),
        grid_spec=pltpu.PrefetchScalarGridSpec(
            num_scalar_prefetch=2,
            grid=(a_pad // MOE_TM, d // tn, nk),
            in_specs=[pl.BlockSpec((MOE_TM, tk), lambda i, j, k, te, nu: (i, k)),
                      pl.BlockSpec((1, tk, tn), lambda i, j, k, te, nu: (te[i], k, j)),
                      pl.BlockSpec((MOE_TM, 1), lambda i, j, k, te, nu: (i, 0))],
            out_specs=pl.BlockSpec((MOE_TM, tn), lambda i, j, k, te, nu: (i, j)),
            scratch_shapes=[pltpu.VMEM((MOE_TM, tn), F32)]),
        out_shape=jax.ShapeDtypeStruct((a_pad, d), BF16),
        compiler_params=_cparams(("parallel", "parallel", "arbitrary")),
        name="moe_down",
    )(tile_expert, n_used, hmid, w_down, row_w)


def _combine_kernel(pos_ref, y_ref, h_ref, gt_ref, gn_ref, o_ref, buf_ref, sem):
    def issue(r, carry):
        pltpu.make_async_copy(y_ref.at[pos_ref[0, 0, r]], buf_ref.at[r], sem).start()
        return carry

    n = buf_ref.shape[0]
    lax.fori_loop(0, n, issue, 0)

    def drain(r, carry):
        pltpu.make_async_copy(y_ref.at[0], buf_ref.at[0], sem).wait()
        return carry

    lax.fori_loop(0, n, drain, 0)
    tt = o_ref.shape[0]
    y = buf_ref[pl.ds(0, tt)].astype(F32) + buf_ref[pl.ds(tt, tt)].astype(F32)
    hn = h_ref[...] + gt_ref[0] * y
    ms = jnp.sum(jnp.sum(hn * hn, axis=2, keepdims=True), axis=1, keepdims=True) * (1.0 / (hn.shape[1] * hn.shape[2]))
    o_ref[...] = hn * lax.rsqrt(ms + EPS) * gn_ref[...]


def _moe_combine_final(y3, pos_tiles, h3, gate3, gain3, tokens_per_batch, tt=256):
    t, sd, _ = h3.shape
    tiles_per_batch = tokens_per_batch // tt
    return pl.pallas_call(
        _combine_kernel,
        grid=(t // tt,),
        in_specs=[pl.BlockSpec((1, 1, TOP_K * tt), lambda i: (i, 0, 0), memory_space=pltpu.SMEM),
                  pl.BlockSpec(memory_space=pl.ANY),
                  pl.BlockSpec((tt, sd, LANES), lambda i: (i, 0, 0)),
                  pl.BlockSpec((1, sd, LANES), lambda i: (i // tiles_per_batch, 0, 0)),
                  pl.BlockSpec((sd, LANES), lambda i: (0, 0))],
        out_specs=pl.BlockSpec((tt, sd, LANES), lambda i: (i, 0, 0)),
        out_shape=jax.ShapeDtypeStruct(h3.shape, F32),
        scratch_shapes=[pltpu.VMEM((TOP_K * tt, sd, LANES), y3.dtype), pltpu.SemaphoreType.DMA(())],
        compiler_params=_cparams(("arbitrary",)),
        name="moe_combine",
    )(pos_tiles, y3, h3, gate3, gain3)


def kernel(x, c, ctx, c_ctx, ada_w, ada_b, norm_mix, norm_ffn, norm_final, gdn_w_in, gdn_conv, gdn_a_log,
           gdn_dt_bias, gdn_o_norm, gdn_w_out, cm_w_in, cm_ln_g, cm_ln_b, cm_w_s, cm_b_s, cm_w_out, ffn_w_gu,
           ffn_w_down, moe_router, moe_router_b, moe_w_gu, moe_w_down):
    b, l, d = x.shape
    n_ctx = ctx.shape[1]
    s = n_ctx + l
    t = b * l
    n_v_heads = gdn_a_log.shape[-1]
    val_dim = n_v_heads * GDN_HEAD_DIM
    key_dim = (gdn_conv.shape[-1] - val_dim) // 2
    conv_dim = 2 * key_dim + val_dim
    ffn_dim = ffn_w_down.shape[1]

    cond8 = jnp.zeros((8, d), F32).at[:b].set(c).at[b].set(c_ctx)
    mods = _ada(cond8, ada_w, ada_b).reshape(ada_w.shape[0], 8, 6, d)

    def seg_mods(layer, first):
        lat = mods[layer, :b, first:first + 2]
        cx = jnp.broadcast_to(mods[layer, b, first:first + 2][None], lat.shape)
        return jnp.stack([cx, lat], axis=1)

    hcat = jnp.concatenate([ctx, x], axis=1)
    u = _norm_mod(hcat, norm_mix[0], seg_mods(0, 0), n_ctx // ROW_TILE).reshape(b * s, d)
    w_in = gdn_w_in[0]
    n_qkvz = conv_dim + val_dim
    tm_cat = _tile(b * s, MM_TM)
    tm = _tile(l, MM_TM)
    proj = _mm(u, w_in, mode="plain", tm=tm_cat, tn=512, tk=d, out_dtype=BF16, n_out=n_qkvz).reshape(b, s, n_qkvz)
    ba = _mm(u, w_in, mode="plain", tm=tm_cat, tn=LANES, tk=d, out_dtype=F32, n_out=LANES, w_col0=n_qkvz)
    qkv = _gdn_conv(proj, gdn_conv[0], n_ctx, key_dim)
    col = _gdn_gates(ba.reshape(b, s, LANES), gdn_a_log[0], gdn_dt_bias[0])
    o_fwd = _gdn_scan(qkv, col, 0, n_ctx, key_dim, n_v_heads)
    o_bwd = _gdn_scan(qkv, col, 1, n_ctx, key_dim, n_v_heads)
    y = _gdn_out_norm(o_fwd, o_bwd, proj, gdn_o_norm[0], n_ctx, conv_dim).reshape(t, val_dim)
    gate = lambda layer, idx: mods[layer, :b, idx].reshape(b, 1, d)
    h = _mm(y, gdn_w_out[0], mode="resid", tm=tm,tn=512, tk=2048, out_dtype=F32, h=x.reshape(t, d),
            gate=gate(0, 2), rows_per_gate=l)

    tf = _norm_mod(h.reshape(b, l, d), norm_ffn[0], seg_mods(0, 3), 0).reshape(t, d)
    mid = _mm(tf, ffn_w_gu[0], mode="swiglu", tm=tm,tn=512, tk=d, out_dtype=BF16, n_out=ffn_dim, w2_col0=ffn_dim)
    h = _mm(mid, ffn_w_down[0], mode="resid", tm=tm,tn=512, tk=1792, out_dtype=F32, h=h, gate=gate(0, 5),
            rows_per_gate=l)

    u = _norm_mod(h.reshape(b, l, d), norm_mix[1], seg_mods(1, 0), 0).reshape(t, d)
    guv = _mm(u, cm_w_in[0], mode="gelu", tm=tm,tn=512, tk=d, out_dtype=BF16)
    cmix = _cm_spatial(guv, cm_ln_g[0], cm_ln_b[0], cm_w_s[0], cm_b_s[0])
    h = _mm(cmix, cm_w_out[0], mode="resid", tm=tm,tn=512, tk=2048, out_dtype=F32, h=h, gate=gate(1, 2),
            rows_per_gate=l)

    tf, idx, wt = _norm_mod_router(h.reshape(b, l, d), norm_ffn[1], seg_mods(1, 3), moe_router[0], moe_router_b[0])
    pos, row_token, row_w, tile_expert, n_used = _moe_plan(idx.reshape(t, LANES)[:, :TOP_K],
                                                           wt.reshape(t, LANES)[:, :TOP_K])
    sd = d // LANES
    xs = _gather_rows(tf.reshape(t, sd, LANES), row_token).reshape(-1, d)
    hmid = _moe_gu(xs, moe_w_gu[0], tile_expert, n_used)
    ys = _moe_down(hmid, moe_w_down[0], row_w, tile_expert, n_used)
    tt = 256
    pos_tiles = jnp.transpose(pos.reshape(t // tt, tt, TOP_K), (0, 2, 1)).reshape(t // tt, 1, TOP_K * tt)
    out3 = _moe_combine_final(ys.reshape(-1, sd, LANES), pos_tiles, h.reshape(t, sd, LANES),
                              mods[1, :b, 5].reshape(b, sd, LANES), norm_final.reshape(sd, LANES), l, tt)
    return out3.reshape(b, l, d)
```

```python
import functools

import jax
import jax.numpy as jnp
from jax import lax
from jax.experimental import pallas as pl
from jax.experimental.pallas import tpu as pltpu

F32 = jnp.float32
BF16 = jnp.bfloat16
EPS = 1e-6
HIGHEST = lax.Precision.HIGHEST

LANES = 128
VMEM_LIMIT_BYTES = 56 * 1024 * 1024

GDN_HEAD_DIM = 128
GDN_CHUNK = 64
GDN_CONV = 5
CM_CHUNK = 128
CM_GROUPS = 16
N_EXPERTS = 8
TOP_K = 2

ROW_TILE = 256
GDN_STEP = 256
GDN_QK_HEADS_PER_STEP = 4
MM_TM = 1024
MOE_TM = 1024
GATHER_ROWS = 512


def _cparams(sem):
    return pltpu.CompilerParams(dimension_semantics=sem, vmem_limit_bytes=VMEM_LIMIT_BYTES)


def _tile(n, preferred):
    t = preferred
    while n % t:
        t //= 2
    return t


def _silu(x):
    return x * (1.0 / (1.0 + jnp.exp(-x)))


def _dot(a, b):
    return jnp.dot(a.astype(BF16), b.astype(BF16), preferred_element_type=F32)


def _dot_nt(a, b):
    return lax.dot_general(a.astype(BF16), b.astype(BF16), (((1,), (1,)), ((), ())),
                           preferred_element_type=F32)


def _ada_kernel(c_ref, w_ref, b_ref, o_ref):
    s = _silu(c_ref[...])
    o_ref[0] = jnp.dot(s, w_ref[0], preferred_element_type=F32, precision=HIGHEST) + b_ref[0]


def _ada(cond8, ada_w, ada_b, tn=1024):
    depth, d, n = ada_w.shape
    return pl.pallas_call(
        _ada_kernel,
        grid=(depth, n // tn),
        in_specs=[pl.BlockSpec((8, d), lambda l, j: (0, 0)),
                  pl.BlockSpec((1, d, tn), lambda l, j: (l, 0, j)),
                  pl.BlockSpec((1, 1, tn), lambda l, j: (l, 0, j))],
        out_specs=pl.BlockSpec((1, 8, tn), lambda l, j: (l, 0, j)),
        out_shape=jax.ShapeDtypeStruct((depth, 8, n), F32),
        compiler_params=_cparams(("parallel", "parallel")),
        name="ada",
    )(cond8, ada_w, ada_b.reshape(depth, 1, n))


def _norm_mod_kernel(h_ref, g_ref, m_ref, o_ref):
    x = h_ref[0]
    y = x * lax.rsqrt(jnp.mean(x * x, axis=-1, keepdims=True) + EPS) * g_ref[...]
    o_ref[0] = (y * (1.0 + m_ref[0, 0, 1:2, :]) + m_ref[0, 0, 0:1, :]).astype(o_ref.dtype)


def _norm_mod(h, gain, mods, n_ctx_tiles):
    b, s, d = h.shape
    return pl.pallas_call(
        _norm_mod_kernel,
        grid=(b, s // ROW_TILE),
        in_specs=[pl.BlockSpec((1, ROW_TILE, d), lambda i, r: (i, r, 0)),
                  pl.BlockSpec((1, d), lambda i, r: (0, 0)),
                  pl.BlockSpec((1, 1, 2, d), lambda i, r: (i, jnp.where(r < n_ctx_tiles, 0, 1), 0, 0))],
        out_specs=pl.BlockSpec((1, ROW_TILE, d), lambda i, r: (i, r, 0)),
        out_shape=jax.ShapeDtypeStruct((b, s, d), BF16),
        compiler_params=_cparams(("parallel", "parallel")),
        name="norm_mod",
    )(h, gain.reshape(1, d), mods)


def _norm_mod_router_kernel(h_ref, g_ref, m_ref, rw_ref, rb_ref, o_ref, idx_ref, wt_ref):
    x = h_ref[0]
    y = x * lax.rsqrt(jnp.mean(x * x, axis=-1, keepdims=True) + EPS) * g_ref[...]
    t = y * (1.0 + m_ref[0, 0, 1:2, :]) + m_ref[0, 0, 0:1, :]
    o_ref[0] = t.astype(o_ref.dtype)
    logits = jnp.dot(t, rw_ref[...], preferred_element_type=F32, precision=HIGHEST) + rb_ref[...]
    lane = lax.broadcasted_iota(jnp.int32, logits.shape, 1)
    m1 = jnp.max(logits, axis=-1, keepdims=True)
    i1 = jnp.min(jnp.where(logits == m1, lane, LANES), axis=-1, keepdims=True)
    rest = jnp.where(lane == i1, -jnp.inf, logits)
    m2 = jnp.max(rest, axis=-1, keepdims=True)
    i2 = jnp.min(jnp.where(rest == m2, lane, LANES), axis=-1, keepdims=True)
    e = jnp.exp(m2 - m1)
    w1 = 1.0 / (1.0 + e)
    w2 = e * w1
    idx_ref[0] = jnp.where(lane == 0, i1, jnp.where(lane == 1, i2, 0))
    wt_ref[0] = jnp.where(lane == 0, w1, jnp.where(lane == 1, w2, 0.0))


def _norm_mod_router(h, gain, mods, router_w, router_b):
    b, s, d = h.shape
    ne = router_w.shape[1]
    rw = jnp.zeros((d, LANES), F32).at[:, :ne].set(router_w)
    rb = jnp.full((1, LANES), -1e30, F32).at[0, :ne].set(router_b)
    return pl.pallas_call(
        _norm_mod_router_kernel,
        grid=(b, s // ROW_TILE),
        in_specs=[pl.BlockSpec((1, ROW_TILE, d), lambda i, r: (i, r, 0)),
                  pl.BlockSpec((1, d), lambda i, r: (0, 0)),
                  pl.BlockSpec((1, 1, 2, d), lambda i, r: (i, 1, 0, 0)),
                  pl.BlockSpec((d, LANES), lambda i, r: (0, 0)),
                  pl.BlockSpec((1, LANES), lambda i, r: (0, 0))],
        out_specs=[pl.BlockSpec((1, ROW_TILE, d), lambda i, r: (i, r, 0)),
                   pl.BlockSpec((1, ROW_TILE, LANES), lambda i, r: (i, r, 0)),
                   pl.BlockSpec((1, ROW_TILE, LANES), lambda i, r: (i, r, 0))],
        out_shape=[jax.ShapeDtypeStruct((b, s, d), BF16),
                   jax.ShapeDtypeStruct((b, s, LANES), jnp.int32),
                   jax.ShapeDtypeStruct((b, s, LANES), F32)],
        compiler_params=_cparams(("parallel", "parallel")),
        name="norm_mod_router",
    )(h, gain.reshape(1, d), mods, rw, rb)


def _gelu_exact(x):
    return 0.5 * x * (1.0 + lax.erf(x * (2.0 ** -0.5)))


def _mm_kernel(*refs, nk, mode):
    n_in = {"swiglu": 3, "resid": 4}.get(mode, 2)
    a_ref, w_ref = refs[:2]
    o_ref = refs[n_in]
    if mode == "swiglu":
        w2_ref = refs[2]
    if mode == "resid":
        h_ref, g_ref = refs[2:4]
    if nk > 1:
        acc_ref = refs[n_in + 1]
        if mode == "swiglu":
            acc2_ref = refs[n_in + 2]
    k = pl.program_id(2)
    a = a_ref[...]
    p = jnp.dot(a, w_ref[...].astype(BF16), preferred_element_type=F32)
    if mode == "swiglu":
        p2 = jnp.dot(a, w2_ref[...].astype(BF16), preferred_element_type=F32)

    if nk > 1:
        @pl.when(k == 0)
        def _():
            acc_ref[...] = p
            if mode == "swiglu":
                acc2_ref[...] = p2

        @pl.when(k > 0)
        def _():
            acc_ref[...] += p
            if mode == "swiglu":
                acc2_ref[...] += p2

    def finish():
        r = acc_ref[...] if nk > 1 else p
        if mode == "plain":
            o_ref[...] = r.astype(o_ref.dtype)
        elif mode == "gelu":
            o_ref[...] = _gelu_exact(r).astype(o_ref.dtype)
        elif mode == "swiglu":
            r2 = acc2_ref[...] if nk > 1 else p2
            o_ref[...] = (_silu(r) * r2).astype(o_ref.dtype)
        else:
            o_ref[...] = h_ref[...] + g_ref[0] * r

    if nk > 1:
        pl.when(k == nk - 1)(finish)
    else:
        finish()


def _mm(a, w, *, mode, tm, tn, tk, out_dtype, n_out=None, w_col0=0, w2_col0=0, h=None, gate=None,
        rows_per_gate=None):
    m, kdim = a.shape
    n_out = w.shape[1] if n_out is None else n_out
    nk = kdim // tk
    assert m % tm == 0 and n_out % tn == 0 and kdim % tk == 0 and w_col0 % tn == 0 and w2_col0 % tn == 0
    cj, cj2 = w_col0 // tn, w2_col0 // tn
    in_specs = [pl.BlockSpec((tm, tk), lambda i, j, k: (i, k)),
                pl.BlockSpec((tk, tn), lambda i, j, k: (k, j + cj))]
    args = [a, w]
    n_acc = 0 if nk == 1 else (2 if mode == "swiglu" else 1)
    scratch = [pltpu.VMEM((tm, tn), F32)] * n_acc
    if mode == "swiglu":
        in_specs.append(pl.BlockSpec((tk, tn), lambda i, j, k: (k, j + cj2)))
        args.append(w)
    if mode == "resid":
        tiles_per_gate = rows_per_gate // tm
        in_specs += [pl.BlockSpec((tm, tn), lambda i, j, k: (i, j)),
                     pl.BlockSpec((1, 1, tn), lambda i, j, k: (i // tiles_per_gate, 0, j))]
        args += [h, gate]
    return pl.pallas_call(
        functools.partial(_mm_kernel, nk=nk, mode=mode),
        grid=(m // tm, n_out // tn, nk),
        in_specs=in_specs,
        out_specs=pl.BlockSpec((tm, tn), lambda i, j, k: (i, j)),
        out_shape=jax.ShapeDtypeStruct((m, n_out), out_dtype),
        scratch_shapes=scratch,
        compiler_params=_cparams(("parallel", "parallel", "arbitrary")),
        name="mm_" + mode,
    )(*args)


def _conv_kernel(x_ref, w_ref, o_ref, *, n_ctx, n_q_blocks, n_k_blocks):
    x = x_ref[0].astype(F32)
    s = x.shape[0]
    row = lax.broadcasted_iota(jnp.int32, x.shape, 0)
    seg_lo = jnp.where(row < n_ctx, 0, n_ctx)
    seg_hi = jnp.where(row < n_ctx, n_ctx, s)
    half = GDN_CONV // 2
    y = x * w_ref[half:half + 1, :]
    for o in range(-half, half + 1):
        if o == 0:
            continue
        shifted = pltpu.roll(x, (-o) % s, 0)
        src = row + o
        ok = (src >= seg_lo) & (src < seg_hi)
        y = y + jnp.where(ok, shifted, 0.0) * w_ref[o + half:o + half + 1, :]
    y = _silu(y)
    cb = pl.program_id(1)
    fac = lax.rsqrt(jnp.sum(y * y, axis=-1, keepdims=True) + EPS)
    q_scale = GDN_HEAD_DIM ** -0.5
    scale = jnp.where(cb < n_q_blocks, fac * q_scale, jnp.where(cb < n_q_blocks + n_k_blocks, fac, 1.0))
    o_ref[0] = (y * scale).astype(o_ref.dtype)


def _gdn_conv(proj, conv_w, n_ctx, key_dim):
    b, s, _ = proj.shape
    c = conv_w.shape[1]
    nqb = key_dim // LANES
    return pl.pallas_call(
        functools.partial(_conv_kernel, n_ctx=n_ctx, n_q_blocks=nqb, n_k_blocks=nqb),
        grid=(b, c // LANES),
        in_specs=[pl.BlockSpec((1, s, LANES), lambda i, j: (i, 0, j)),
                  pl.BlockSpec((GDN_CONV, LANES), lambda i, j: (0, j))],
        out_specs=pl.BlockSpec((1, s, LANES), lambda i, j: (i, 0, j)),
        out_shape=jax.ShapeDtypeStruct((b, s, c), BF16),
        compiler_params=_cparams(("parallel", "parallel")),
        name="gdn_conv",
    )(proj, conv_w)


def _gates_kernel(ba_ref, alog_ref, dtb_ref, o_ref, *, n_dir_lanes):
    x = ba_ref[0]
    lc = lax.broadcasted_iota(jnp.int32, (GDN_CHUNK, LANES), 1)
    half = 2 * n_dir_lanes
    beta = 1.0 / (1.0 + jnp.exp(-x))
    z = x + dtb_ref[...]
    softplus = jnp.maximum(z, 0.0) + jnp.log1p(jnp.exp(-jnp.abs(z)))
    g = -jnp.exp(alog_ref[...]) * softplus
    ri = lax.broadcasted_iota(jnp.int32, (GDN_CHUNK, GDN_CHUNK), 0)
    ci = lax.broadcasted_iota(jnp.int32, (GDN_CHUNK, GDN_CHUNK), 1)
    lower = (ri >= ci).astype(F32)
    upper = (ri <= ci).astype(F32)
    for c in range(x.shape[0] // GDN_CHUNK):
        rows = slice(c * GDN_CHUNK, (c + 1) * GDN_CHUNK)
        gc = g[rows]
        fwd = jnp.dot(lower, gc, preferred_element_type=F32, precision=HIGHEST)
        bwd = jnp.dot(upper, gc, preferred_element_type=F32, precision=HIGHEST)
        gcs =jnp.where(lc < half + n_dir_lanes, fwd, bwd)
        o_ref[0, rows, :] = jnp.where(lc < half, beta[rows], gcs)


def _gdn_gates(ba, a_log, dt_bias):
    b, s, _ = ba.shape
    nd = a_log.shape[-1]
    zeros = jnp.zeros((2 * nd,), F32)
    alog = jnp.concatenate([zeros, a_log.reshape(-1)]).reshape(1, LANES)
    dtb = jnp.concatenate([zeros, dt_bias.reshape(-1)]).reshape(1, LANES)
    return pl.pallas_call(
        functools.partial(_gates_kernel, n_dir_lanes=nd),
        grid=(b, s // GDN_STEP),
        in_specs=[pl.BlockSpec((1, GDN_STEP, LANES), lambda i, r: (i, r, 0)),
                  pl.BlockSpec((1, LANES), lambda i, r: (0, 0)),
                  pl.BlockSpec((1, LANES), lambda i, r: (0, 0))],
        out_specs=pl.BlockSpec((1, GDN_STEP, LANES), lambda i, r: (i, r, 0)),
        out_shape=jax.ShapeDtypeStruct((b, s, LANES), F32),
        compiler_params=_cparams(("parallel", "parallel")),
        name="gdn_gates",
    )(ba, alog, dtb)


def _gdn_kernel(q_ref, k_ref, v_ref, col_ref, o_ref, s_ref, *, direction, n_dir_lanes):
    hg = pl.program_id(1)
    step = pl.program_id(2)
    hq = GDN_QK_HEADS_PER_STEP
    nv = 2 * hq
    c = GDN_CHUNK
    dh = GDN_HEAD_DIM
    n_chunks = GDN_STEP // c

    @pl.when(step == 0)
    def _():
        s_ref[...] = jnp.zeros_like(s_ref)

    col = col_ref[0]
    lane0 = direction * n_dir_lanes + hg * nv
    beta_g = pltpu.roll(col, (LANES - lane0) % LANES, 1)
    gcs_g = pltpu.roll(col, (LANES // 2 - lane0) % LANES, 1)
    gcs_r = gcs_g.T

    ri = lax.broadcasted_iota(jnp.int32, (c, c), 0)
    ci = lax.broadcasted_iota(jnp.int32, (c, c), 1)
    if direction == 0:
        incl, strict = ri >= ci, ri > ci
    else:
        incl, strict = ri <= ci, ri < ci
    eye = (ri == ci).astype(F32)
    blockdiag = (ri // 16) == (ci // 16)
    eye_d = (lax.broadcasted_iota(jnp.int32, (dh, dh), 0) == lax.broadcasted_iota(jnp.int32, (dh, dh), 1)).astype(BF16)

    order = range(n_chunks) if direction == 0 else range(n_chunks - 1, -1, -1)
    heads = range(nv)
    for p in order:
        rows = slice(p * c, (p + 1) * c)
        ks = [k_ref[0, rows, qh * dh:(qh + 1) * dh] for qh in range(hq)]
        qs = [q_ref[0, rows, qh * dh:(qh + 1) * dh] for qh in range(hq)]
        vs = [v_ref[0, rows, j * dh:(j + 1) * dh] for j in heads]
        kk = [_dot_nt(k, k) for k in ks]
        qk = [_dot_nt(q, k) for q, k in zip(qs, ks)]
        k_t = [_dot_nt(eye_d, k) for k in ks]
        bcol = [beta_g[rows, j:j + 1] for j in heads]
        gcol = [gcs_g[rows, j:j + 1] for j in heads]
        grow = [gcs_r[j:j + 1, rows] for j in heads]
        glast = [g[:, c - 1:c] if direction == 0 else g[:, 0:1] for g in grow]
        decay = [jnp.exp(jnp.where(incl, gcol[j] - grow[j], -jnp.inf)) for j in heads]
        a = [jnp.where(strict, kk[j // 2] * bcol[j] * decay[j], 0.0) for j in heads]
        a_d = [jnp.where(blockdiag, x, 0.0) for x in a]
        a_o = [x - y for x, y in zip(a, a_d)]
        p2 = [_dot(x, x) for x in a_d]
        p4 = [_dot(x, x) for x in p2]
        p8 = [_dot(x, x) for x in p4]
        t1 = [_dot(eye - x, eye + y) for x, y in zip(a_d, p2)]
        t2 = [_dot(x, eye + y) for x, y in zip(t1, p4)]
        t_d = [_dot(x, eye + y) for x, y in zip(t2, p8)]
        nn = [_dot(x, y) for x, y in zip(t_d, a_o)]
        n2 = [_dot(x, x) for x in nn]
        rr = [_dot(eye - x, eye + y) for x, y in zip(nn, n2)]
        t = [_dot(x, y) for x, y in zip(rr, t_d)]
        rhs = [jnp.concatenate([vs[j].astype(F32) * bcol[j],
                                ks[j // 2].astype(F32) * (bcol[j] * jnp.exp(gcol[j]))], axis=1) for j in heads]
        uw = [_dot(x, y) for x, y in zip(t, rhs)]
        state = [s_ref[j] for j in heads]
        wq = [_dot(jnp.concatenate([uw[j][:, dh:], qs[j // 2].astype(F32) * jnp.exp(gcol[j])], axis=0), state[j])
              for j in heads]
        v_new = [uw[j][:, :dh] - wq[j][:c] for j in heads]
        lhs2 = [jnp.concatenate([jnp.where(incl, qk[j // 2] * decay[j], 0.0),
                                 k_t[j // 2] * jnp.exp(glast[j] - grow[j])], axis=0) for j in heads]
        r2 = [_dot(x, y) for x, y in zip(lhs2, v_new)]
        for j in heads:
            o_ref[0, rows, j * dh:(j + 1) * dh] = (wq[j][c:] + r2[j][:c]).astype(o_ref.dtype)
            s_ref[j] = state[j] * jnp.exp(glast[j]) + r2[j][c:]


def _gdn_scan(qkv, col, direction, n_ctx, key_dim, n_v_heads):
    b, s, _ = qkv.shape
    hq = GDN_QK_HEADS_PER_STEP
    nv = 2 * hq
    n_steps = s // GDN_STEP
    n_ctx_steps = n_ctx // GDN_STEP
    assert n_ctx_steps == 1
    qw = hq * GDN_HEAD_DIM
    vw = nv * GDN_HEAD_DIM
    n_groups = key_dim // qw
    k_blk0 = key_dim // qw
    v_blk0 = 2 * key_dim // vw

    def blk(st):
        if direction == 0:
            return st
        return jnp.where(st == 0, 0, n_steps - st)

    def oblk(st):
        return blk(jnp.maximum(st, 1)) - 1

    return pl.pallas_call(
        functools.partial(_gdn_kernel, direction=direction, n_dir_lanes=n_v_heads),
        grid=(b, n_groups, n_steps),
        in_specs=[pl.BlockSpec((1, GDN_STEP, qw), lambda i, g, st: (i, blk(st), g)),
                  pl.BlockSpec((1, GDN_STEP, qw), lambda i, g, st: (i, blk(st), k_blk0 + g)),
                  pl.BlockSpec((1, GDN_STEP, vw), lambda i, g, st: (i, blk(st), v_blk0 + g)),
                  pl.BlockSpec((1, GDN_STEP, LANES), lambda i, g, st: (i, blk(st), 0))],
        out_specs=pl.BlockSpec((1, GDN_STEP, vw), lambda i, g, st: (i, oblk(st), g)),
        out_shape=jax.ShapeDtypeStruct((b, s - n_ctx, n_v_heads * GDN_HEAD_DIM), BF16),
        scratch_shapes=[pltpu.VMEM((nv, GDN_HEAD_DIM, GDN_HEAD_DIM), F32)],
        compiler_params=_cparams(("parallel", "parallel", "arbitrary")),
        name="gdn_scan_d%d" % direction,
    )(qkv, qkv, qkv, col)


def _gdn_out_kernel(o0_ref, o1_ref, z_ref, g_ref, y_ref):
    dh = GDN_HEAD_DIM
    for hd in range(o0_ref.shape[2] // dh):
        sl = slice(hd * dh, (hd + 1) * dh)
        o = o0_ref[0, :, sl].astype(F32) + o1_ref[0, :, sl].astype(F32)
        y = o * lax.rsqrt(jnp.mean(o * o, axis=-1, keepdims=True) + EPS) * g_ref[...]
        y_ref[0, :, sl] = (y * _silu(z_ref[0, :, sl].astype(F32))).astype(y_ref.dtype)


def _gdn_out_norm(o0, o1, proj, o_norm, n_ctx, z_col0):
    b, l, w = o0.shape
    ctx_tiles = n_ctx // ROW_TILE
    zb = z_col0 // w
    return pl.pallas_call(
        _gdn_out_kernel,
        grid=(b, l // ROW_TILE),
        in_specs=[pl.BlockSpec((1, ROW_TILE, w), lambda i, r: (i, r, 0)),
                  pl.BlockSpec((1, ROW_TILE, w), lambda i, r: (i, r, 0)),
                  pl.BlockSpec((1, ROW_TILE, w), lambda i, r: (i, r + ctx_tiles, zb)),
                  pl.BlockSpec((1, GDN_HEAD_DIM), lambda i, r: (0, 0))],
        out_specs=pl.BlockSpec((1, ROW_TILE, w), lambda i, r: (i, r, 0)),
        out_shape=jax.ShapeDtypeStruct((b, l, w), BF16),
        compiler_params=_cparams(("parallel", "parallel")),
        name="gdn_out_norm",
    )(o0, o1, proj, o_norm.reshape(1, -1))


def _cm_kernel(gu_ref, gv_ref, lg_ref, lb_ref, ws_ref, bs_ref, o_ref):
    gd = gu_ref.shape[1] // CM_GROUPS
    for ch in range(gu_ref.shape[0] // CM_CHUNK):
        rows = slice(ch * CM_CHUNK, (ch + 1) * CM_CHUNK)
        gv = gv_ref[rows, :].astype(F32)
        mu = jnp.mean(gv, axis=-1, keepdims=True)
        xc = gv - mu
        var = jnp.mean(xc * xc, axis=-1, keepdims=True)
        gvn = xc * lax.rsqrt(var + EPS) * lg_ref[...] + lb_ref[...]
        for g in range(CM_GROUPS):
            sl = slice(g * gd, (g + 1) * gd)
            mixed = _dot(ws_ref[g], gvn[:, sl]) + bs_ref[:, g:g + 1]
            o_ref[rows, sl] = (gu_ref[rows, sl].astype(F32) * mixed).astype(o_ref.dtype)


def _cm_spatial(guv, ln_g, ln_b, w_s, b_s):
    t, w2 = guv.shape
    w = w2 // 2
    rt = 2 * CM_CHUNK
    return pl.pallas_call(
        _cm_kernel,
        grid=(t // rt,),
        in_specs=[pl.BlockSpec((rt, w), lambda i: (i, 0)),
                  pl.BlockSpec((rt, w), lambda i: (i, 1)),
                  pl.BlockSpec((1, w), lambda i: (0, 0)),
                  pl.BlockSpec((1, w), lambda i: (0, 0)),
                  pl.BlockSpec((CM_GROUPS, CM_CHUNK, CM_CHUNK), lambda i: (0, 0, 0)),
                  pl.BlockSpec((CM_CHUNK, CM_GROUPS), lambda i: (0, 0))],
        out_specs=pl.BlockSpec((rt, w), lambda i: (i, 0)),
        out_shape=jax.ShapeDtypeStruct((t, w), BF16),
        compiler_params=_cparams(("parallel",)),
        name="cm_spatial",
    )(guv, guv, ln_g.reshape(1, w), ln_b.reshape(1, w), w_s, jnp.transpose(b_s))


def _moe_plan(idx2, wt2):
    t = idx2.shape[0]
    a = t * TOP_K
    e = idx2.reshape(a)
    onehot = (e[:, None] == jnp.arange(N_EXPERTS, dtype=jnp.int32)[None, :]).astype(jnp.int32)
    rank = jnp.sum((jnp.cumsum(onehot, axis=0) - onehot) * onehot, axis=1)
    counts = jnp.sum(onehot, axis=0)
    padded = ((counts + MOE_TM - 1) // MOE_TM) * MOE_TM
    ends = jnp.cumsum(padded)
    starts = ends - padded
    pos = starts[e] + rank
    a_pad = a + N_EXPERTS * MOE_TM
    row_token = jnp.zeros((a_pad,), jnp.int32).at[pos].set(jnp.arange(a, dtype=jnp.int32) // TOP_K)
    row_w = jnp.zeros((a_pad,), F32).at[pos].set(wt2.reshape(a))
    n_tiles = a_pad // MOE_TM
    tile_start = jnp.arange(n_tiles, dtype=jnp.int32) * MOE_TM
    tile_expert = jnp.minimum(jnp.searchsorted(ends, tile_start, side="right"), N_EXPERTS - 1).astype(jnp.int32)
    n_used = (ends[-1] // MOE_TM).astype(jnp.int32).reshape(1)
    return pos.reshape(t, TOP_K), row_token, row_w.reshape(a_pad, 1), tile_expert, n_used


def _gather_kernel(idx_ref, src_ref, dst_ref, sem):
    def issue(r, carry):
        pltpu.make_async_copy(src_ref.at[idx_ref[0, 0, r]], dst_ref.at[r], sem).start()
        return carry

    lax.fori_loop(0, GATHER_ROWS, issue, 0)

    def drain(r, carry):
        pltpu.make_async_copy(src_ref.at[0], dst_ref.at[0], sem).wait()
        return carry

    lax.fori_loop(0, GATHER_ROWS, drain, 0)


def _gather_rows(src3, row_idx):
    r = row_idx.shape[0]
    nb = r // GATHER_ROWS
    return pl.pallas_call(
        _gather_kernel,
        grid=(nb,),
        in_specs=[pl.BlockSpec((1, 1, GATHER_ROWS), lambda i: (i, 0, 0), memory_space=pltpu.SMEM),
                  pl.BlockSpec(memory_space=pl.ANY)],
        out_specs=pl.BlockSpec((GATHER_ROWS,) + src3.shape[1:], lambda i: (i, 0, 0)),
        out_shape=jax.ShapeDtypeStruct((r,) + src3.shape[1:], src3.dtype),
        scratch_shapes=[pltpu.SemaphoreType.DMA(())],
        compiler_params=_cparams(("arbitrary",)),
        name="gather_rows",
    )(row_idx.reshape(nb, 1, GATHER_ROWS), src3)


def _moe_gu_kernel(te_ref, nu_ref, x_ref, wg_ref, wu_ref, o_ref):
    used = pl.program_id(0) < nu_ref[0]

    @pl.when(used)
    def _():
        x = x_ref[...]
        g = jnp.dot(x, wg_ref[0].astype(BF16), preferred_element_type=F32)
        u = jnp.dot(x, wu_ref[0].astype(BF16), preferred_element_type=F32)
        o_ref[...] = (_silu(g) * u).astype(o_ref.dtype)

    @pl.when(jnp.logical_not(used))
    def _():
        o_ref[...] = jnp.zeros_like(o_ref)


def _moe_gu(xs, w_gu, tile_expert, n_used, tn=512):
    a_pad, d = xs.shape
    f = w_gu.shape[2] // 2
    nj = f // tn
    return pl.pallas_call(
        _moe_gu_kernel,
        grid_spec=pltpu.PrefetchScalarGridSpec(
            num_scalar_prefetch=2,
            grid=(a_pad // MOE_TM, nj),
            in_specs=[pl.BlockSpec((MOE_TM, d), lambda i, j, te, nu: (i, 0)),
                      pl.BlockSpec((1, d, tn), lambda i, j, te, nu: (te[i], 0, j)),
                      pl.BlockSpec((1, d, tn), lambda i, j, te, nu: (te[i], 0, j + nj))],
            out_specs=pl.BlockSpec((MOE_TM, tn), lambda i, j, te, nu: (i, j))),
        out_shape=jax.ShapeDtypeStruct((a_pad, f), BF16),
        compiler_params=_cparams(("parallel", "parallel")),
        name="moe_gu",
    )(tile_expert, n_used, xs, w_gu, w_gu)


def _moe_down_kernel(te_ref, nu_ref, a_ref, w_ref, rw_ref, o_ref, acc_ref, *, nk):
    k = pl.program_id(2)
    used = pl.program_id(0) < nu_ref[0]

    @pl.when(jnp.logical_not(used) & (k == nk - 1))
    def _():
        o_ref[...] = jnp.zeros_like(o_ref)

    @pl.when(used)
    def _():
        p = jnp.dot(a_ref[...], w_ref[0].astype(BF16), preferred_element_type=F32)

        @pl.when(k == 0)
        def _():
            acc_ref[...] = p

        @pl.when(k > 0)
        def _():
            acc_ref[...] += p

        @pl.when(k == nk - 1)
        def _():
            o_ref[...] = (acc_ref[...] * rw_ref[...]).astype(o_ref.dtype)


def _moe_down(hmid, w_down, row_w, tile_expert, n_used, tn=1024, tk=1792):
    a_pad, f = hmid.shape
    d = w_down.shape[2]
    nk = f // tk
    return pl.pallas_call(
        functools.partial(_moe_down_kernel, nk=nk),
        grid_spec=pltpu.PrefetchScalarGridSpec(
            num_scalar_prefetch=2,
            grid=(a_pad // MOE_TM, d // tn, nk),
            in_specs=[pl.BlockSpec((MOE_TM, tk), lambda i, j, k, te, nu: (i, k)),
                      pl.BlockSpec((1, tk, tn), lambda i, j, k, te, nu: (te[i], k, j)),
                      pl.BlockSpec((MOE_TM, 1), lambda i, j, k, te, nu: (i, 0))],
            out_specs=pl.BlockSpec((MOE_TM, tn), lambda i, j, k, te, nu: (i, j)),
            scratch_shapes=[pltpu.VMEM((MOE_TM, tn), F32)]),
        out_shape=jax.ShapeDtypeStruct((a_pad, d), BF16),
        compiler_params=_cparams(("parallel", "parallel", "arbitrary")),
        name="moe_down",
    )(tile_expert, n_used, hmid, w_down, row_w)


def _combine_kernel(pos_ref, y_ref, h_ref, gt_ref, gn_ref, o_ref, buf_ref, sem):
    def issue(r, carry):
        pltpu.make_async_copy(y_ref.at[pos_ref[0, 0, r]], buf_ref.at[r], sem).start()
        return carry

    n = buf_ref.shape[0]
    lax.fori_loop(0, n, issue, 0)

    def drain(r, carry):
        pltpu.make_async_copy(y_ref.at[0], buf_ref.at[0], sem).wait()
        return carry

    lax.fori_loop(0, n, drain, 0)
    tt = o_ref.shape[0]
    y = buf_ref[pl.ds(0, tt)].astype(F32) + buf_ref[pl.ds(tt, tt)].astype(F32)
    hn = h_ref[...] + gt_ref[0] * y
    ms = jnp.sum(jnp.sum(hn * hn, axis=2, keepdims=True), axis=1, keepdims=True) * (1.0 / (hn.shape[1] * hn.shape[2]))
    o_ref[...] = hn * lax.rsqrt(ms + EPS) * gn_ref[...]


def _moe_combine_final(y3, pos_tiles, h3, gate3, gain3, tokens_per_batch, tt=256):
    t, sd, _ = h3.shape
    tiles_per_batch = tokens_per_batch // tt
    return pl.pallas_call(
        _combine_kernel,
        grid=(t // tt,),
        in_specs=[pl.BlockSpec((1, 1, TOP_K * tt), lambda i: (i, 0, 0), memory_space=pltpu.SMEM),
                  pl.BlockSpec(memory_space=pl.ANY),
                  pl.BlockSpec((tt, sd, LANES), lambda i: (i, 0, 0)),
                  pl.BlockSpec((1, sd, LANES), lambda i: (i // tiles_per_batch, 0, 0)),
                  pl.BlockSpec((sd, LANES), lambda i: (0, 0))],
        out_specs=pl.BlockSpec((tt, sd, LANES), lambda i: (i, 0, 0)),
        out_shape=jax.ShapeDtypeStruct(h3.shape, F32),
        scratch_shapes=[pltpu.VMEM((TOP_K * tt, sd, LANES), y3.dtype), pltpu.SemaphoreType.DMA(())],
        compiler_params=_cparams(("arbitrary",)),
        name="moe_combine",
    )(pos_tiles, y3, h3, gate3, gain3)


def kernel(x, c, ctx, c_ctx, ada_w, ada_b, norm_mix, norm_ffn, norm_final, gdn_w_in, gdn_conv, gdn_a_log,
           gdn_dt_bias, gdn_o_norm, gdn_w_out, cm_w_in, cm_ln_g, cm_ln_b, cm_w_s, cm_b_s, cm_w_out, ffn_w_gu,
           ffn_w_down, moe_router, moe_router_b, moe_w_gu, moe_w_down):
    b, l, d = x.shape
    n_ctx = ctx.shape[1]
    s = n_ctx + l
    t = b * l
    n_v_heads = gdn_a_log.shape[-1]
    val_dim = n_v_heads * GDN_HEAD_DIM
    key_dim = (gdn_conv.shape[-1] - val_dim) // 2
    conv_dim = 2 * key_dim + val_dim
    ffn_dim = ffn_w_down.shape[1]

    cond8 = jnp.zeros((8, d), F32).at[:b].set(c).at[b].set(c_ctx)
    mods = _ada(cond8, ada_w, ada_b).reshape(ada_w.shape[0], 8, 6, d)

    def seg_mods(layer, first):
        lat = mods[layer, :b, first:first + 2]
        cx = jnp.broadcast_to(mods[layer, b, first:first + 2][None], lat.shape)
        return jnp.stack([cx, lat], axis=1)

    hcat = jnp.concatenate([ctx, x], axis=1)
    u = _norm_mod(hcat, norm_mix[0], seg_mods(0, 0), n_ctx // ROW_TILE).reshape(b * s, d)
    w_in = gdn_w_in[0]
    n_qkvz = conv_dim + val_dim
    tm_cat = _tile(b * s, MM_TM)
    tm = _tile(l, MM_TM)
    proj = _mm(u, w_in, mode="plain", tm=tm_cat, tn=512, tk=d, out_dtype=BF16, n_out=n_qkvz).reshape(b, s, n_qkvz)
    ba = _mm(u, w_in, mode="plain", tm=tm_cat, tn=LANES, tk=d, out_dtype=F32, n_out=LANES, w_col0=n_qkvz)
    qkv = _gdn_conv(proj, gdn_conv[0], n_ctx, key_dim)
    col = _gdn_gates(ba.reshape(b, s, LANES), gdn_a_log[0], gdn_dt_bias[0])
    o_fwd = _gdn_scan(qkv, col, 0, n_ctx, key_dim, n_v_heads)
    o_bwd = _gdn_scan(qkv, col, 1, n_ctx, key_dim, n_v_heads)
    y = _gdn_out_norm(o_fwd, o_bwd, proj, gdn_o_norm[0], n_ctx, conv_dim).reshape(t, val_dim)
    gate = lambda layer, idx: mods[layer, :b, idx].reshape(b, 1, d)
    h = _mm(y, gdn_w_out[0], mode="resid", tm=tm,tn=512, tk=2048, out_dtype=F32, h=x.reshape(t, d),
            gate=gate(0, 2), rows_per_gate=l)

    tf = _norm_mod(h.reshape(b, l, d), norm_ffn[0], seg_mods(0, 3), 0).reshape(t, d)
    mid = _mm(tf, ffn_w_gu[0], mode="swiglu", tm=tm,tn=512, tk=d, out_dtype=BF16, n_out=ffn_dim, w2_col0=ffn_dim)
    h = _mm(mid, ffn_w_down[0], mode="resid", tm=tm,tn=512, tk=1792, out_dtype=F32, h=h, gate=gate(0, 5),
            rows_per_gate=l)

    u = _norm_mod(h.reshape(b, l, d), norm_mix[1], seg_mods(1, 0), 0).reshape(t, d)
    guv = _mm(u, cm_w_in[0], mode="gelu", tm=tm,tn=512, tk=d, out_dtype=BF16)
    cmix = _cm_spatial(guv, cm_ln_g[0], cm_ln_b[0], cm_w_s[0], cm_b_s[0])
    h = _mm(cmix, cm_w_out[0], mode="resid", tm=tm,tn=512, tk=2048, out_dtype=F32, h=h, gate=gate(1, 2),
            rows_per_gate=l)

    tf, idx, wt = _norm_mod_router(h.reshape(b, l, d), norm_ffn[1], seg_mods(1, 3), moe_router[0], moe_router_b[0])
    pos, row_token, row_w, tile_expert, n_used = _moe_plan(idx.reshape(t, LANES)[:, :TOP_K],
                                                           wt.reshape(t, LANES)[:, :TOP_K])
    sd = d // LANES
    xs = _gather_rows(tf.reshape(t, sd, LANES), row_token).reshape(-1, d)
    hmid = _moe_gu(xs, moe_w_gu[0], tile_expert, n_used)
    ys = _moe_down(hmid, moe_w_down[0], row_w, tile_expert, n_used)
    tt = 256
    pos_tiles = jnp.transpose(pos.reshape(t // tt, tt, TOP_K), (0, 2, 1)).reshape(t // tt, 1, TOP_K * tt)
    out3 = _moe_combine_final(ys.reshape(-1, sd, LANES), pos_tiles, h.reshape(t, sd, LANES),
                              mods[1, :b, 5].reshape(b, sd, LANES), norm_final.reshape(sd, LANES), l, tt)
    return out3.reshape(b, l, d)
```

```python
import functools

import jax
import jax.numpy as jnp
from jax import lax
from jax.experimental import pallas as pl
from jax.experimental.pallas import tpu as pltpu

F32 = jnp.float32
BF16 = jnp.bfloat16
EPS = 1e-6
HIGHEST = lax.Precision.HIGHEST

LANES = 128
VMEM_LIMIT_BYTES = 56 * 1024 * 1024

GDN_HEAD_DIM = 128
GDN_CHUNK = 64
GDN_CONV = 5
CONV_PAD_ROWS = 8
CM_CHUNK = 128
CM_GROUPS = 16
N_EXPERTS = 8
TOP_K = 2

ROW_TILE = 256
GDN_STEP = 256
GDN_QK_HEADS_PER_STEP = 4
GDN_CHUNKS_PER_STAGE = 4
MM_TM = 1024
MOE_TM = 1024
GATHER_ROWS = 512
DMA_UNROLL = 8


def _cparams(sem):
    return pltpu.CompilerParams(dimension_semantics=sem, vmem_limit_bytes=VMEM_LIMIT_BYTES)


def _tile(n, preferred):
    t = preferred
    while n % t:
        t //= 2
    return t


def _silu(x):
    return x * (1.0 / (1.0 + jnp.exp(-x)))


def _dot(a, b):
    return jnp.dot(a.astype(BF16), b.astype(BF16), preferred_element_type=F32)


def _dot_nt(a, b):
    return lax.dot_general(a.astype(BF16), b.astype(BF16), (((1,), (1,)), ((), ())),
                           preferred_element_type=F32)


def _ada_kernel(c_ref, w_ref, b_ref, o_ref):
    s = _silu(c_ref[...])
    o_ref[0] = jnp.dot(s, w_ref[0], preferred_element_type=F32, precision=HIGHEST) + b_ref[0]


def _ada(cond8, ada_w, ada_b, tn=1024):
    depth, d, n = ada_w.shape
    return pl.pallas_call(
        _ada_kernel,
        grid=(depth, n // tn),
        in_specs=[pl.BlockSpec((8, d), lambda l, j: (0, 0)),
                  pl.BlockSpec((1, d, tn), lambda l, j: (l, 0, j)),
                  pl.BlockSpec((1, 1, tn), lambda l, j: (l, 0, j))],
        out_specs=pl.BlockSpec((1, 8, tn), lambda l, j: (l, 0, j)),
        out_shape=jax.ShapeDtypeStruct((depth, 8, n), F32),
        compiler_params=_cparams(("parallel", "parallel")),
        name="ada",
    )(cond8, ada_w, ada_b.reshape(depth, 1, n))


def _norm_mod_kernel(h_ref, g_ref, m_ref, o_ref):
    x = h_ref[0]
    y = x * lax.rsqrt(jnp.mean(x * x, axis=-1, keepdims=True) + EPS) * g_ref[...]
    o_ref[0] = (y * (1.0 + m_ref[0, 0, 1:2, :]) + m_ref[0, 0, 0:1, :]).astype(o_ref.dtype)


def _norm_mod(h, gain, mods, n_ctx_tiles):
    b, s, d = h.shape
    return pl.pallas_call(
        _norm_mod_kernel,
        grid=(b, s // ROW_TILE),
        in_specs=[pl.BlockSpec((1, ROW_TILE, d), lambda i, r: (i, r, 0)),
                  pl.BlockSpec((1, d), lambda i, r: (0, 0)),
                  pl.BlockSpec((1, 1, 2, d), lambda i, r: (i, jnp.where(r < n_ctx_tiles, 0, 1), 0, 0))],
        out_specs=pl.BlockSpec((1, ROW_TILE, d), lambda i, r: (i, r, 0)),
        out_shape=jax.ShapeDtypeStruct((b, s, d), BF16),
        compiler_params=_cparams(("parallel", "parallel")),
        name="norm_mod",
    )(h, gain.reshape(1, d), mods)


def _norm_mod_router_kernel(h_ref, g_ref, m_ref, rw_ref, rb_ref, o_ref, idx_ref, wt_ref):
    x = h_ref[0]
    y = x * lax.rsqrt(jnp.mean(x * x, axis=-1, keepdims=True) + EPS) * g_ref[...]
    t = y * (1.0 + m_ref[0, 0, 1:2, :]) + m_ref[0, 0, 0:1, :]
    for j in range(o_ref.shape[2]):
        o_ref[0, :, j, :] = t[:, j * LANES:(j + 1) * LANES].astype(o_ref.dtype)
    logits = jnp.dot(t, rw_ref[...], preferred_element_type=F32, precision=HIGHEST) + rb_ref[...]
    lane = lax.broadcasted_iota(jnp.int32, logits.shape, 1)
    m1 = jnp.max(logits, axis=-1, keepdims=True)
    i1 = jnp.min(jnp.where(logits == m1, lane, LANES), axis=-1, keepdims=True)
    rest = jnp.where(lane == i1, -jnp.inf, logits)
    m2 = jnp.max(rest, axis=-1, keepdims=True)
    i2 = jnp.min(jnp.where(rest == m2, lane, LANES), axis=-1, keepdims=True)
    e = jnp.exp(m2 - m1)
    w1 = 1.0 / (1.0 + e)
    w2 = e * w1
    idx_ref[0] = jnp.where(lane == 0, i1, jnp.where(lane == 1, i2, 0))
    wt_ref[0] = jnp.where(lane == 0, w1, jnp.where(lane == 1, w2, 0.0))


def _norm_mod_router(h, gain, mods, router_w, router_b):
    b, s, d = h.shape
    ne = router_w.shape[1]
    rw = jnp.zeros((d, LANES), F32).at[:, :ne].set(router_w)
    rb = jnp.full((1, LANES), -1e30, F32).at[0, :ne].set(router_b)
    return pl.pallas_call(
        _norm_mod_router_kernel,
        grid=(b, s // ROW_TILE),
        in_specs=[pl.BlockSpec((1, ROW_TILE, d), lambda i, r: (i, r, 0)),
                  pl.BlockSpec((1, d), lambda i, r: (0, 0)),
                  pl.BlockSpec((1, 1, 2, d), lambda i, r: (i, 1, 0, 0)),
                  pl.BlockSpec((d, LANES), lambda i, r: (0, 0)),
                  pl.BlockSpec((1, LANES), lambda i, r: (0, 0))],
        out_specs=[pl.BlockSpec((1, ROW_TILE, d // LANES, LANES), lambda i, r: (i, r, 0, 0)),
                   pl.BlockSpec((1, ROW_TILE, LANES), lambda i, r: (i, r, 0)),
                   pl.BlockSpec((1, ROW_TILE, LANES), lambda i, r: (i, r, 0))],
        out_shape=[jax.ShapeDtypeStruct((b, s, d // LANES, LANES), BF16),
                   jax.ShapeDtypeStruct((b, s, LANES), jnp.int32),
                   jax.ShapeDtypeStruct((b, s, LANES), F32)],
        compiler_params=_cparams(("parallel", "parallel")),
        name="norm_mod_router",
    )(h, gain.reshape(1, d), mods, rw, rb)


def _gelu_exact(x):
    return 0.5 * x * (1.0 + lax.erf(x * (2.0 ** -0.5)))


def _mm_kernel(*refs, nk, mode):
    n_in = {"swiglu": 3, "resid": 4}.get(mode, 2)
    a_ref, w_ref = refs[:2]
    o_ref = refs[n_in]
    if mode == "swiglu":
        w2_ref = refs[2]
    if mode == "resid":
        h_ref, g_ref = refs[2:4]
    if nk > 1:
        acc_ref = refs[n_in + 1]
        if mode == "swiglu":
            acc2_ref = refs[n_in + 2]
    k = pl.program_id(2)
    a = a_ref[...]
    p = jnp.dot(a, w_ref[...].astype(BF16), preferred_element_type=F32)
    if mode == "swiglu":
        p2 = jnp.dot(a, w2_ref[...].astype(BF16), preferred_element_type=F32)

    if nk > 1:
        @pl.when(k == 0)
        def _():
            acc_ref[...] = p
            if mode == "swiglu":
                acc2_ref[...] = p2

        @pl.when(k > 0)
        def _():
            acc_ref[...] += p
            if mode == "swiglu":
                acc2_ref[...] += p2

    def finish():
        r = acc_ref[...] if nk > 1 else p
        if mode == "plain":
            o_ref[...] = r.astype(o_ref.dtype)
        elif mode == "gelu":
            o_ref[...] = _gelu_exact(r).astype(o_ref.dtype)
        elif mode == "swiglu":
            r2 = acc2_ref[...] if nk > 1 else p2
            o_ref[...] = (_silu(r) * r2).astype(o_ref.dtype)
        else:
            o_ref[...] = h_ref[...] + g_ref[0] * r

    if nk > 1:
        pl.when(k == nk - 1)(finish)
    else:
        finish()


def _mm(a, w, *, mode, tm, tn, tk, out_dtype, n_out=None, w_col0=0, w2_col0=0, h=None, gate=None,
        rows_per_gate=None):
    m, kdim = a.shape
    n_out = w.shape[1] if n_out is None else n_out
    nk = kdim // tk
    assert m % tm == 0 and n_out % tn == 0 and kdim % tk == 0 and w_col0 % tn == 0 and w2_col0 % tn == 0
    cj, cj2 = w_col0 // tn, w2_col0 // tn
    in_specs = [pl.BlockSpec((tm, tk), lambda i, j, k: (i, k)),
                pl.BlockSpec((tk, tn), lambda i, j, k: (k, j + cj))]
    args = [a, w]
    n_acc = 0 if nk == 1 else (2 if mode == "swiglu" else 1)
    scratch = [pltpu.VMEM((tm, tn), F32)] * n_acc
    if mode == "swiglu":
        in_specs.append(pl.BlockSpec((tk, tn), lambda i, j, k: (k, j + cj2)))
        args.append(w)
    if mode == "resid":
        tiles_per_gate = rows_per_gate // tm
        in_specs += [pl.BlockSpec((tm, tn), lambda i, j, k: (i, j)),
                     pl.BlockSpec((1, 1, tn), lambda i, j, k: (i // tiles_per_gate, 0, j))]
        args += [h, gate]
    return pl.pallas_call(
        functools.partial(_mm_kernel, nk=nk, mode=mode),
        grid=(m // tm, n_out // tn, nk),
        in_specs=in_specs,
        out_specs=pl.BlockSpec((tm, tn), lambda i, j, k: (i, j)),
        out_shape=jax.ShapeDtypeStruct((m, n_out), out_dtype),
        scratch_shapes=scratch,
        compiler_params=_cparams(("parallel", "parallel", "arbitrary")),
        name="mm_" + mode,
    )(*args)


def _conv_kernel(x_ref, w_ref, o_ref, xp_ref, *, n_ctx, n_q_blocks, n_k_blocks):
    s = x_ref.shape[1]
    half = GDN_CONV // 2
    w = [w_ref[i:i + 1, :] for i in range(GDN_CONV)]
    cb = pl.program_id(1)
    q_scale = GDN_HEAD_DIM ** -0.5
    pad = CONV_PAD_ROWS
    zeros = jnp.zeros((pad, LANES), F32)
    base = pad
    for lo, hi in ((0, n_ctx), (n_ctx, s)):
        n = hi - lo
        xp_ref[base - pad:base, :] = zeros
        xp_ref[base:base + n, :] = x_ref[0, lo:hi, :].astype(F32)
        xp_ref[base + n:base + n + pad, :] = zeros
        y = xp_ref[base:base + n, :] * w[half]
        for o in range(-half, half + 1):
            if o != 0:
                y = y + xp_ref[base + o:base + o + n, :] * w[o + half]
        base += n + pad
        y = _silu(y)
        fac = lax.rsqrt(jnp.sum(y * y, axis=-1, keepdims=True) + EPS)
        scale = jnp.where(cb < n_q_blocks, fac * q_scale, jnp.where(cb < n_q_blocks + n_k_blocks, fac, 1.0))
        o_ref[0, lo:hi, :] = (y * scale).astype(o_ref.dtype)


def _gdn_conv(proj, conv_w, n_ctx, key_dim):
    b, s, _ = proj.shape
    c = conv_w.shape[1]
    nqb = key_dim // LANES
    return pl.pallas_call(
        functools.partial(_conv_kernel, n_ctx=n_ctx, n_q_blocks=nqb, n_k_blocks=nqb),
        grid=(b, c // LANES),
        in_specs=[pl.BlockSpec((1, s, LANES), lambda i, j: (i, 0, j)),
                  pl.BlockSpec((GDN_CONV, LANES), lambda i, j: (0, j))],
        out_specs=pl.BlockSpec((1, s, LANES), lambda i, j: (i, 0, j)),
        out_shape=jax.ShapeDtypeStruct((b, s, c), BF16),
        scratch_shapes=[pltpu.VMEM((s + 3 * CONV_PAD_ROWS, LANES), F32)],
        compiler_params=_cparams(("parallel", "parallel")),
        name="gdn_conv",
    )(proj, conv_w)


def _gates_kernel(ba_ref, alog_ref, dtb_ref, o_ref, *, n_dir_lanes):
    x = ba_ref[0]
    lc = lax.broadcasted_iota(jnp.int32, (GDN_CHUNK, LANES), 1)
    half = 2 * n_dir_lanes
    beta = 1.0 / (1.0 + jnp.exp(-x))
    z = x + dtb_ref[...]
    softplus = jnp.maximum(z, 0.0) + jnp.log1p(jnp.exp(-jnp.abs(z)))
    g = -jnp.exp(alog_ref[...]) * softplus
    ri = lax.broadcasted_iota(jnp.int32, (GDN_CHUNK, GDN_CHUNK), 0)
    ci = lax.broadcasted_iota(jnp.int32, (GDN_CHUNK, GDN_CHUNK), 1)
    lower = (ri >= ci).astype(F32)
    upper = (ri <= ci).astype(F32)
    for c in range(x.shape[0] // GDN_CHUNK):
        rows = slice(c * GDN_CHUNK, (c + 1) * GDN_CHUNK)
        gc = g[rows]
        fwd = jnp.dot(lower, gc, preferred_element_type=F32, precision=HIGHEST)
        bwd = jnp.dot(upper, gc, preferred_element_type=F32, precision=HIGHEST)
        gcs =jnp.where(lc < half + n_dir_lanes, fwd, bwd)
        o_ref[0, rows, :] = jnp.where(lc < half, beta[rows], gcs)


def _gdn_gates(ba, a_log, dt_bias):
    b, s, _ = ba.shape
    nd = a_log.shape[-1]
    zeros = jnp.zeros((2 * nd,), F32)
    alog = jnp.concatenate([zeros, a_log.reshape(-1)]).reshape(1, LANES)
    dtb = jnp.concatenate([zeros, dt_bias.reshape(-1)]).reshape(1, LANES)
    return pl.pallas_call(
        functools.partial(_gates_kernel, n_dir_lanes=nd),
        grid=(b, s // GDN_STEP),
        in_specs=[pl.BlockSpec((1, GDN_STEP, LANES), lambda i, r: (i, r, 0)),
                  pl.BlockSpec((1, LANES), lambda i, r: (0, 0)),
                  pl.BlockSpec((1, LANES), lambda i, r: (0, 0))],
        out_specs=pl.BlockSpec((1, GDN_STEP, LANES), lambda i, r: (i, r, 0)),
        out_shape=jax.ShapeDtypeStruct((b, s, LANES), F32),
        compiler_params=_cparams(("parallel", "parallel")),
        name="gdn_gates",
    )(ba, alog, dtb)


def _gdn_kernel(q_ref, k_ref, v_ref, col_ref, o_ref, s_ref, *, direction, n_dir_lanes):
    hg = pl.program_id(1)
    step = pl.program_id(2)
    hq = GDN_QK_HEADS_PER_STEP
    nv = 2 * hq
    c = GDN_CHUNK
    dh = GDN_HEAD_DIM
    n_chunks = GDN_STEP // c

    @pl.when(step == 0)
    def _():
        s_ref[...] = jnp.zeros_like(s_ref)

    col = col_ref[0]
    lane0 = direction * n_dir_lanes + hg * nv
    beta_g = pltpu.roll(col, (LANES - lane0) % LANES, 1)
    gcs_g = pltpu.roll(col, (LANES // 2 - lane0) % LANES, 1)
    gcs_r = gcs_g.T

    ri = lax.broadcasted_iota(jnp.int32, (c, c), 0)
    ci = lax.broadcasted_iota(jnp.int32, (c, c), 1)
    if direction == 0:
        incl, strict = ri >= ci, ri > ci
    else:
        incl, strict = ri <= ci, ri < ci
    eye = (ri == ci).astype(F32)
    blockdiag = (ri // 16) == (ci // 16)
    order = range(n_chunks) if direction == 0 else range(n_chunks - 1, -1, -1)
    heads = range(nv)
    order = list(order)
    groups = [order[i:i + GDN_CHUNKS_PER_STAGE] for i in range(0, n_chunks, GDN_CHUNKS_PER_STAGE)]
    for grp in groups:
        rows = {p: slice(p * c, (p + 1) * c) for p in grp}
        pq = [(p, qh) for p in grp for qh in range(hq)]
        pj = [(p, j) for p in grp for j in heads]
        ks = {(p, qh): k_ref[0, rows[p], qh * dh:(qh + 1) * dh] for p, qh in pq}
        qs = {(p, qh): q_ref[0, rows[p], qh * dh:(qh + 1) * dh] for p, qh in pq}
        kk = {x: _dot_nt(ks[x], ks[x]) for x in pq}
        qk = {x: _dot_nt(qs[x], ks[x]) for x in pq}
        k_t = {x: ks[x].astype(F32).T for x in pq}
        bcol = {(p, j): beta_g[rows[p], j:j + 1] for p, j in pj}
        gcol = {(p, j): gcs_g[rows[p], j:j + 1] for p, j in pj}
        grow = {(p, j): gcs_r[j:j + 1, rows[p]] for p, j in pj}
        glast = {x: (g[:, c - 1:c] if direction == 0 else g[:, 0:1]) for x, g in grow.items()}
        decay = {x: jnp.exp(jnp.where(incl, gcol[x] - grow[x], -jnp.inf)) for x in pj}
        a = {(p, j): jnp.where(strict, kk[p, j // 2] * bcol[p, j] * decay[p, j], 0.0) for p, j in pj}
        a_d = {x: jnp.where(blockdiag, a[x], 0.0) for x in pj}
        a_o = {x: a[x] - a_d[x] for x in pj}
        p2 = {x: _dot(a_d[x], a_d[x]) for x in pj}
        p4 = {x: _dot(p2[x], p2[x]) for x in pj}
        p8 = {x: _dot(p4[x], p4[x]) for x in pj}
        t1 = {x: _dot(eye - a_d[x], eye + p2[x]) for x in pj}
        t2 = {x: _dot(t1[x], eye + p4[x]) for x in pj}
        t_d = {x: _dot(t2[x], eye + p8[x]) for x in pj}
        nn = {x: _dot(t_d[x], a_o[x]) for x in pj}
        n2 = {x: _dot(nn[x], nn[x]) for x in pj}
        rr = {x: _dot(eye - nn[x], eye + n2[x]) for x in pj}
        t = {x: _dot(rr[x], t_d[x]) for x in pj}
        rhs = {(p, j): jnp.concatenate([v_ref[0, rows[p], j * dh:(j + 1) * dh].astype(F32) * bcol[p, j],
                                        ks[p, j // 2].astype(F32) * (bcol[p, j] * jnp.exp(gcol[p, j]))], axis=1)
               for p, j in pj}
        uw = {x: _dot(t[x], rhs[x]) for x in pj}
        wq_lhs = {(p, j): jnp.concatenate([uw[p, j][:, dh:], qs[p, j // 2].astype(F32) * jnp.exp(gcol[p, j])], axis=0)
                  for p, j in pj}
        lhs2 = {(p, j): jnp.concatenate([jnp.where(incl, qk[p, j // 2] * decay[p, j], 0.0),
                                         k_t[p, j // 2] * jnp.exp(glast[p, j] - grow[p, j])], axis=0)
                for p, j in pj}
        for p in grp:
            state = [s_ref[j] for j in heads]
            wq = [_dot(wq_lhs[p, j], state[j]) for j in heads]
            v_new = [uw[p, j][:, :dh] - wq[j][:c] for j in heads]
            r2 = [_dot(lhs2[p, j], v_new[j]) for j in heads]
            for j in heads:
                o_ref[0, rows[p], j * dh:(j + 1) * dh] = (wq[j][c:] + r2[j][:c]).astype(o_ref.dtype)
                s_ref[j] = state[j] * jnp.exp(glast[p, j]) + r2[j][c:]


def _gdn_scan(qkv, col, direction, n_ctx, key_dim, n_v_heads):
    b, s, _ = qkv.shape
    hq = GDN_QK_HEADS_PER_STEP
    nv = 2 * hq
    n_steps = s // GDN_STEP
    n_ctx_steps = n_ctx // GDN_STEP
    assert n_ctx_steps == 1
    qw = hq * GDN_HEAD_DIM
    vw = nv * GDN_HEAD_DIM
    n_groups = key_dim // qw
    k_blk0 = key_dim // qw
    v_blk0 = 2 * key_dim // vw

    def blk(st):
        if direction == 0:
            return st
        return jnp.where(st == 0, 0, n_steps - st)

    def oblk(st):
        return blk(jnp.maximum(st, 1)) - 1

    return pl.pallas_call(
        functools.partial(_gdn_kernel, direction=direction, n_dir_lanes=n_v_heads),
        grid=(b, n_groups, n_steps),
        in_specs=[pl.BlockSpec((1, GDN_STEP, qw), lambda i, g, st: (i, blk(st), g)),
                  pl.BlockSpec((1, GDN_STEP, qw), lambda i, g, st: (i, blk(st), k_blk0 + g)),
                  pl.BlockSpec((1, GDN_STEP, vw), lambda i, g, st: (i, blk(st), v_blk0 + g)),
                  pl.BlockSpec((1, GDN_STEP, LANES), lambda i, g, st: (i, blk(st), 0))],
        out_specs=pl.BlockSpec((1, GDN_STEP, vw), lambda i, g, st: (i, oblk(st), g)),
        out_shape=jax.ShapeDtypeStruct((b, s - n_ctx, n_v_heads * GDN_HEAD_DIM), BF16),
        scratch_shapes=[pltpu.VMEM((nv, GDN_HEAD_DIM, GDN_HEAD_DIM), F32)],
        compiler_params=_cparams(("parallel", "parallel", "arbitrary")),
        name="gdn_scan_d%d" % direction,
    )(qkv, qkv, qkv, col)


def _gdn_out_kernel(o0_ref, o1_ref, z_ref, g_ref, y_ref):
    dh = GDN_HEAD_DIM
    for hd in range(o0_ref.shape[2] // dh):
        sl = slice(hd * dh, (hd + 1) * dh)
        o = o0_ref[0, :, sl].astype(F32) + o1_ref[0, :, sl].astype(F32)
        y = o * lax.rsqrt(jnp.mean(o * o, axis=-1, keepdims=True) + EPS) * g_ref[...]
        y_ref[0, :, sl] = (y * _silu(z_ref[0, :, sl].astype(F32))).astype(y_ref.dtype)


def _gdn_out_norm(o0, o1, proj, o_norm, n_ctx, z_col0):
    b, l, w = o0.shape
    ctx_tiles = n_ctx // ROW_TILE
    zb = z_col0 // w
    return pl.pallas_call(
        _gdn_out_kernel,
        grid=(b, l // ROW_TILE),
        in_specs=[pl.BlockSpec((1, ROW_TILE, w), lambda i, r: (i, r, 0)),
                  pl.BlockSpec((1, ROW_TILE, w), lambda i, r: (i, r, 0)),
                  pl.BlockSpec((1, ROW_TILE, w), lambda i, r: (i, r + ctx_tiles, zb)),
                  pl.BlockSpec((1, GDN_HEAD_DIM), lambda i, r: (0, 0))],
        out_specs=pl.BlockSpec((1, ROW_TILE, w), lambda i, r: (i, r, 0)),
        out_shape=jax.ShapeDtypeStruct((b, l, w), BF16),
        compiler_params=_cparams(("parallel", "parallel")),
        name="gdn_out_norm",
    )(o0, o1, proj, o_norm.reshape(1, -1))


def _cm_kernel(gu_ref, gv_ref, lg_ref, lb_ref, ws_ref, bs_ref, o_ref):
    gd = gu_ref.shape[1] // CM_GROUPS
    for ch in range(gu_ref.shape[0] // CM_CHUNK):
        rows = slice(ch * CM_CHUNK, (ch + 1) * CM_CHUNK)
        gv = gv_ref[rows, :].astype(F32)
        mu = jnp.mean(gv, axis=-1, keepdims=True)
        xc = gv - mu
        var = jnp.mean(xc * xc, axis=-1, keepdims=True)
        gvn = xc * lax.rsqrt(var + EPS) * lg_ref[...] + lb_ref[...]
        for g in range(CM_GROUPS):
            sl = slice(g * gd, (g + 1) * gd)
            mixed = _dot(ws_ref[g], gvn[:, sl]) + bs_ref[:, g:g + 1]
            o_ref[rows, sl] = (gu_ref[rows, sl].astype(F32) * mixed).astype(o_ref.dtype)


def _cm_spatial(guv, ln_g, ln_b, w_s, b_s):
    t, w2 = guv.shape
    w = w2 // 2
    rt = 2 * CM_CHUNK
    return pl.pallas_call(
        _cm_kernel,
        grid=(t // rt,),
        in_specs=[pl.BlockSpec((rt, w), lambda i: (i, 0)),
                  pl.BlockSpec((rt, w), lambda i: (i, 1)),
                  pl.BlockSpec((1, w), lambda i: (0, 0)),
                  pl.BlockSpec((1, w), lambda i: (0, 0)),
                  pl.BlockSpec((CM_GROUPS, CM_CHUNK, CM_CHUNK), lambda i: (0, 0, 0)),
                  pl.BlockSpec((CM_CHUNK, CM_GROUPS), lambda i: (0, 0))],
        out_specs=pl.BlockSpec((rt, w), lambda i: (i, 0)),
        out_shape=jax.ShapeDtypeStruct((t, w), BF16),
        compiler_params=_cparams(("parallel",)),
        name="cm_spatial",
    )(guv, guv, ln_g.reshape(1, w), ln_b.reshape(1, w), w_s, jnp.transpose(b_s))


def _moe_plan(idx2):
    t = idx2.shape[0]
    a = t * TOP_K
    e = idx2.reshape(a)
    onehot = (e[:, None] == jnp.arange(N_EXPERTS, dtype=jnp.int32)[None, :]).astype(jnp.int32)
    rank = jnp.sum((jnp.cumsum(onehot, axis=0) - onehot) * onehot, axis=1)
    counts = jnp.sum(onehot, axis=0)
    padded = ((counts + MOE_TM - 1) // MOE_TM) * MOE_TM
    ends = jnp.cumsum(padded)
    starts = ends - padded
    pos = starts[e] + rank
    a_pad = a + N_EXPERTS * MOE_TM
    row_token = jnp.zeros((a_pad,), jnp.int32).at[pos].set(jnp.arange(a, dtype=jnp.int32) // TOP_K)
    n_tiles = a_pad // MOE_TM
    tile_start = jnp.arange(n_tiles, dtype=jnp.int32) * MOE_TM
    tile_expert = jnp.minimum(jnp.searchsorted(ends, tile_start, side="right"), N_EXPERTS - 1).astype(jnp.int32)
    n_used = (ends[-1] // MOE_TM).astype(jnp.int32).reshape(1)
    return pos.reshape(t, TOP_K), row_token, tile_expert, n_used


def _gather_kernel(idx_ref, src_ref, dst_ref, sem):
    def issue(r, carry):
        pltpu.make_async_copy(src_ref.at[idx_ref[0, 0, r]], dst_ref.at[r], sem).start()
        return carry

    lax.fori_loop(0, GATHER_ROWS, issue, 0, unroll=DMA_UNROLL)

    def drain(r, carry):
        pltpu.make_async_copy(src_ref.at[0], dst_ref.at[0], sem).wait()
        return carry

    lax.fori_loop(0, GATHER_ROWS, drain, 0, unroll=DMA_UNROLL)


def _gather_rows(src3, row_idx):
    r = row_idx.shape[0]
    nb = r // GATHER_ROWS
    return pl.pallas_call(
        _gather_kernel,
        grid=(nb,),
        in_specs=[pl.BlockSpec((1, 1, GATHER_ROWS), lambda i: (i, 0, 0), memory_space=pltpu.SMEM),
                  pl.BlockSpec(memory_space=pl.ANY)],
        out_specs=pl.BlockSpec((GATHER_ROWS,) + src3.shape[1:], lambda i: (i, 0, 0)),
        out_shape=jax.ShapeDtypeStruct((r,) + src3.shape[1:], src3.dtype),
        scratch_shapes=[pltpu.SemaphoreType.DMA(())],
        compiler_params=_cparams(("arbitrary",)),
        name="gather_rows",
    )(row_idx.reshape(nb, 1, GATHER_ROWS), src3)


def _moe_gu_kernel(te_ref, nu_ref, x_ref, wg_ref, wu_ref, o_ref, x2_ref):
    used = pl.program_id(0) < nu_ref[0]

    @pl.when(used & (pl.program_id(1) == 0))
    def _():
        for s in range(x_ref.shape[1]):
            x2_ref[:, s * LANES:(s + 1) * LANES] = x_ref[:, s, :]

    @pl.when(used)
    def _():
        x = x2_ref[...]
        g = jnp.dot(x, wg_ref[0].astype(BF16), preferred_element_type=F32)
        u = jnp.dot(x, wu_ref[0].astype(BF16), preferred_element_type=F32)
        o_ref[...] = (_silu(g) * u).astype(o_ref.dtype)

    @pl.when(jnp.logical_not(used))
    def _():
        o_ref[...] = jnp.zeros_like(o_ref)


def _moe_gu(xs3, w_gu, tile_expert, n_used, tn=512):
    a_pad, sd, _ = xs3.shape
    d = sd * LANES
    f = w_gu.shape[2] // 2
    nj = f // tn
    return pl.pallas_call(
        _moe_gu_kernel,
        grid_spec=pltpu.PrefetchScalarGridSpec(
            num_scalar_prefetch=2,
            grid=(a_pad // MOE_TM, nj),
            in_specs=[pl.BlockSpec((MOE_TM, sd, LANES), lambda i, j, te, nu: (i, 0, 0)),
                      pl.BlockSpec((1, d, tn), lambda i, j, te, nu: (te[i], 0, j)),
                      pl.BlockSpec((1, d, tn), lambda i, j, te, nu: (te[i], 0, j + nj))],
            out_specs=pl.BlockSpec((MOE_TM, tn), lambda i, j, te, nu: (i, j)),
            scratch_shapes=[pltpu.VMEM((MOE_TM, d), BF16)]),
        out_shape=jax.ShapeDtypeStruct((a_pad, f), BF16),
        compiler_params=_cparams(("parallel", "arbitrary")),
        name="moe_gu",
    )(tile_expert, n_used, xs3, w_gu, w_gu)


def _moe_down_kernel(te_ref, nu_ref, a_ref, w_ref, o_ref, acc_ref, *, nk):
    k = pl.program_id(1)
    used = pl.program_id(0) < nu_ref[0]

    @pl.when(jnp.logical_not(used) & (k == nk - 1))
    def _():
        o_ref[...] = jnp.zeros_like(o_ref)

    @pl.when(used)
    def _():
        p = jnp.dot(a_ref[...], w_ref[0].astype(BF16), preferred_element_type=F32)

        @pl.when(k == 0)
        def _():
            acc_ref[...] = p

        @pl.when(k > 0)
        def _():
            acc_ref[...] += p

        @pl.when(k == nk - 1)
        def _():
            for s in range(o_ref.shape[1]):
                o_ref[:, s, :] = acc_ref[:, s * LANES:(s + 1) * LANES].astype(o_ref.dtype)


def _moe_down(hmid, w_down, tile_expert, n_used, tk=896):
    a_pad, f = hmid.shape
    d = w_down.shape[2]
    sd = d // LANES
    nk = f // tk
    return pl.pallas_call(
        functools.partial(_moe_down_kernel, nk=nk),
        grid_spec=pltpu.PrefetchScalarGridSpec(
            num_scalar_prefetch=2,
            grid=(a_pad // MOE_TM, nk),
            in_specs=[pl.BlockSpec((MOE_TM, tk), lambda i, k, te, nu: (i, k)),
                      pl.BlockSpec((1, tk, d), lambda i, k, te, nu: (te[i], k, 0))],
            out_specs=pl.BlockSpec((MOE_TM, sd, LANES), lambda i, k, te, nu: (i, 0, 0)),
            scratch_shapes=[pltpu.VMEM((MOE_TM, d), F32)]),
        out_shape=jax.ShapeDtypeStruct((a_pad, sd, LANES), BF16),
        compiler_params=_cparams(("parallel", "arbitrary")),
        name="moe_down",
    )(tile_expert, n_used, hmid, w_down)


def _combine_kernel(pos_ref, y_ref, wt_ref, h_ref, gt_ref, gn_ref, o_ref, buf_ref, sem):
    def issue(r, carry):
        pltpu.make_async_copy(y_ref.at[pos_ref[0, 0, r]], buf_ref.at[r], sem).start()
        return carry

    n = buf_ref.shape[0]
    lax.fori_loop(0, n, issue, 0, unroll=DMA_UNROLL)

    def drain(r, carry):
        pltpu.make_async_copy(y_ref.at[0], buf_ref.at[0], sem).wait()
        return carry

    lax.fori_loop(0, n, drain, 0, unroll=DMA_UNROLL)
    tt = o_ref.shape[0]
    w0 = wt_ref[:, 0:1]
    w1 = wt_ref[:, 1:2]
    ss = jnp.zeros((tt, 1), F32)
    for s in range(buf_ref.shape[1]):
        sl = slice(s * LANES, (s + 1) * LANES)
        y = w0 * buf_ref[pl.ds(0, tt), s, :].astype(F32) + w1 * buf_ref[pl.ds(tt, tt), s, :].astype(F32)
        hn = h_ref[:, sl] + gt_ref[0, :, sl] * y
        ss = ss + jnp.sum(hn * hn, axis=-1, keepdims=True)
        o_ref[:, sl] = hn
    o_ref[...] = o_ref[...] * lax.rsqrt(ss * (1.0 / o_ref.shape[1]) + EPS) * gn_ref[...]


def _moe_combine_final(y3, pos_tiles, wt, h, gate, gain, tokens_per_batch, tt=256):
    t, d = h.shape
    sd = d // LANES
    tiles_per_batch = tokens_per_batch // tt
    return pl.pallas_call(
        _combine_kernel,
        grid=(t // tt,),
        in_specs=[pl.BlockSpec((1, 1, TOP_K * tt), lambda i: (i, 0, 0), memory_space=pltpu.SMEM),
                  pl.BlockSpec(memory_space=pl.ANY),
                  pl.BlockSpec((tt, LANES), lambda i: (i, 0)),
                  pl.BlockSpec((tt, d), lambda i: (i, 0)),
                  pl.BlockSpec((1, 1, d), lambda i: (i // tiles_per_batch, 0, 0)),
                  pl.BlockSpec((1, d), lambda i: (0, 0))],
        out_specs=pl.BlockSpec((tt, d), lambda i: (i, 0)),
        out_shape=jax.ShapeDtypeStruct((t, d), F32),
        scratch_shapes=[pltpu.VMEM((TOP_K * tt, sd, LANES), y3.dtype), pltpu.SemaphoreType.DMA(())],
        compiler_params=_cparams(("arbitrary",)),
        name="moe_combine",
    )(pos_tiles, y3, wt, h, gate, gain.reshape(1, d))


def kernel(x, c, ctx, c_ctx, ada_w, ada_b, norm_mix, norm_ffn, norm_final, gdn_w_in, gdn_conv, gdn_a_log,
           gdn_dt_bias, gdn_o_norm, gdn_w_out, cm_w_in, cm_ln_g, cm_ln_b, cm_w_s, cm_b_s, cm_w_out, ffn_w_gu,
           ffn_w_down, moe_router, moe_router_b, moe_w_gu, moe_w_down):
    b, l, d = x.shape
    n_ctx = ctx.shape[1]
    s = n_ctx + l
    t = b * l
    n_v_heads = gdn_a_log.shape[-1]
    val_dim = n_v_heads * GDN_HEAD_DIM
    key_dim = (gdn_conv.shape[-1] - val_dim) // 2
    conv_dim = 2 * key_dim + val_dim
    ffn_dim = ffn_w_down.shape[1]

    cond8 = jnp.zeros((8, d), F32).at[:b].set(c).at[b].set(c_ctx)
    mods = _ada(cond8, ada_w, ada_b).reshape(ada_w.shape[0], 8, 6, d)

    def seg_mods(layer, first):
        lat = mods[layer, :b, first:first + 2]
        cx = jnp.broadcast_to(mods[layer, b, first:first + 2][None], lat.shape)
        return jnp.stack([cx, lat], axis=1)

    hcat = jnp.concatenate([ctx, x], axis=1)
    u = _norm_mod(hcat, norm_mix[0], seg_mods(0, 0), n_ctx // ROW_TILE).reshape(b * s, d)
    w_in = gdn_w_in[0]
    n_qkvz = conv_dim + val_dim
    tm_cat = _tile(b * s, MM_TM)
    tm = _tile(l, MM_TM)
    proj = _mm(u, w_in, mode="plain", tm=tm_cat, tn=1024, tk=d, out_dtype=BF16, n_out=n_qkvz).reshape(b, s, n_qkvz)
    ba = _mm(u, w_in, mode="plain", tm=tm_cat, tn=LANES, tk=d, out_dtype=F32, n_out=LANES, w_col0=n_qkvz)
    qkv = _gdn_conv(proj, gdn_conv[0], n_ctx, key_dim)
    col = _gdn_gates(ba.reshape(b, s, LANES), gdn_a_log[0], gdn_dt_bias[0])
    o_fwd = _gdn_scan(qkv, col, 0, n_ctx, key_dim, n_v_heads)
    o_bwd = _gdn_scan(qkv, col, 1, n_ctx, key_dim, n_v_heads)
    y = _gdn_out_norm(o_fwd, o_bwd, proj, gdn_o_norm[0], n_ctx, conv_dim).reshape(t, val_dim)
    gate = lambda layer, idx: mods[layer, :b, idx].reshape(b, 1, d)
    h = _mm(y, gdn_w_out[0], mode="resid", tm=tm,tn=512, tk=val_dim, out_dtype=F32, h=x.reshape(t, d),
            gate=gate(0, 2), rows_per_gate=l)

    tf = _norm_mod(h.reshape(b, l, d), norm_ffn[0], seg_mods(0, 3), 0).reshape(t, d)
    mid = _mm(tf, ffn_w_gu[0], mode="swiglu", tm=tm,tn=512, tk=d, out_dtype=BF16, n_out=ffn_dim, w2_col0=ffn_dim)
    h = _mm(mid, ffn_w_down[0], mode="resid", tm=tm,tn=256, tk=ffn_dim, out_dtype=F32, h=h, gate=gate(0, 5),
            rows_per_gate=l)

    u = _norm_mod(h.reshape(b, l, d), norm_mix[1], seg_mods(1, 0), 0).reshape(t, d)
    guv = _mm(u, cm_w_in[0], mode="gelu", tm=tm, tn=1024, tk=d, out_dtype=BF16)
    cmix = _cm_spatial(guv, cm_ln_g[0], cm_ln_b[0], cm_w_s[0], cm_b_s[0])
    h = _mm(cmix, cm_w_out[0], mode="resid", tm=tm,tn=512, tk=cmix.shape[1], out_dtype=F32, h=h, gate=gate(1, 2),
            rows_per_gate=l)

    tf, idx, wt = _norm_mod_router(h.reshape(b, l, d), norm_ffn[1], seg_mods(1, 3), moe_router[0], moe_router_b[0])
    pos, row_token, tile_expert, n_used = _moe_plan(idx.reshape(t, LANES)[:, :TOP_K])
    xs3 = _gather_rows(tf.reshape(t, d // LANES, LANES), row_token)
    hmid = _moe_gu(xs3, moe_w_gu[0], tile_expert, n_used)
    ys3 = _moe_down(hmid, moe_w_down[0], tile_expert, n_used)
    tt = 256
    pos_tiles = jnp.transpose(pos.reshape(t // tt, tt, TOP_K), (0, 2, 1)).reshape(t // tt, 1, TOP_K * tt)
    out = _moe_combine_final(ys3, pos_tiles, wt.reshape(t, LANES), h, gate(1, 5), norm_final, l, tt)
    return out.reshape(b, l, d)
```

```python
import functools

import jax
import jax.numpy as jnp
from jax import lax
from jax.experimental import pallas as pl
from jax.experimental.pallas import tpu as pltpu

F32 = jnp.float32
BF16 = jnp.bfloat16
EPS = 1e-6
HIGHEST = lax.Precision.HIGHEST

LANES = 128
VMEM_LIMIT_BYTES = 56 * 1024 * 1024

GDN_HEAD_DIM = 128
GDN_CHUNK = 64
GDN_CONV = 5
CONV_PAD_ROWS = 8
CM_CHUNK = 128
CM_GROUPS = 16
N_EXPERTS = 8
TOP_K = 2

ROW_TILE = 256
GDN_STEP = 256
GDN_QK_HEADS_PER_STEP = 8
GDN_CHUNKS_PER_STAGE = 4
MM_TM = 1024
MOE_TM = 1024
MOE_SUB = 256
GATHER_ROWS = 512
DMA_UNROLL = 8


def _cparams(sem):
    return pltpu.CompilerParams(dimension_semantics=sem, vmem_limit_bytes=VMEM_LIMIT_BYTES)


def _tile(n, preferred):
    t = preferred
    while n % t:
        t //= 2
    return t


def _silu(x):
    return x * (1.0 / (1.0 + jnp.exp(-x)))


def _dot(a, b):
    return jnp.dot(a.astype(BF16), b.astype(BF16), preferred_element_type=F32)


def _dot_nt(a, b):
    return lax.dot_general(a.astype(BF16), b.astype(BF16), (((1,), (1,)), ((), ())),
                           preferred_element_type=F32)


def _ada_kernel(c_ref, w_ref, b_ref, o_ref):
    s = _silu(c_ref[...])
    o_ref[0] = jnp.dot(s, w_ref[0], preferred_element_type=F32, precision=HIGHEST) + b_ref[0]


def _ada(cond8, ada_w, ada_b, tn=1024):
    depth, d, n = ada_w.shape
    return pl.pallas_call(
        _ada_kernel,
        grid=(depth, n // tn),
        in_specs=[pl.BlockSpec((8, d), lambda l, j: (0, 0)),
                  pl.BlockSpec((1, d, tn), lambda l, j: (l, 0, j)),
                  pl.BlockSpec((1, 1, tn), lambda l, j: (l, 0, j))],
        out_specs=pl.BlockSpec((1, 8, tn), lambda l, j: (l, 0, j)),
        out_shape=jax.ShapeDtypeStruct((depth, 8, n), F32),
        compiler_params=_cparams(("parallel", "parallel")),
        name="ada",
    )(cond8, ada_w, ada_b.reshape(depth, 1, n))


def _norm_mod_kernel(h_ref, g_ref, m_ref, o_ref):
    x = h_ref[0]
    y = x * lax.rsqrt(jnp.mean(x * x, axis=-1, keepdims=True) + EPS) * g_ref[...]
    o_ref[0] = (y * (1.0 + m_ref[0, 0, 1:2, :]) + m_ref[0, 0, 0:1, :]).astype(o_ref.dtype)


def _norm_mod(h, gain, mods, n_ctx_tiles):
    b, s, d = h.shape
    return pl.pallas_call(
        _norm_mod_kernel,
        grid=(b, s // ROW_TILE),
        in_specs=[pl.BlockSpec((1, ROW_TILE, d), lambda i, r: (i, r, 0)),
                  pl.BlockSpec((1, d), lambda i, r: (0, 0)),
                  pl.BlockSpec((1, 1, 2, d), lambda i, r: (i, jnp.where(r < n_ctx_tiles, 0, 1), 0, 0))],
        out_specs=pl.BlockSpec((1, ROW_TILE, d), lambda i, r: (i, r, 0)),
        out_shape=jax.ShapeDtypeStruct((b, s, d), BF16),
        compiler_params=_cparams(("parallel", "parallel")),
        name="norm_mod",
    )(h, gain.reshape(1, d), mods)


def _norm_mod_router_kernel(h_ref, g_ref, m_ref, rw_ref, rb_ref, o_ref, idx_ref, wt_ref):
    x = h_ref[0]
    y = x * lax.rsqrt(jnp.mean(x * x, axis=-1, keepdims=True) + EPS) * g_ref[...]
    t = y * (1.0 + m_ref[0, 0, 1:2, :]) + m_ref[0, 0, 0:1, :]
    o_ref[0] = t.astype(o_ref.dtype)
    logits = jnp.dot(t, rw_ref[...], preferred_element_type=F32, precision=HIGHEST) + rb_ref[...]
    lane = lax.broadcasted_iota(jnp.int32, logits.shape, 1)
    m1 = jnp.max(logits, axis=-1, keepdims=True)
    i1 = jnp.min(jnp.where(logits == m1, lane, LANES), axis=-1, keepdims=True)
    rest = jnp.where(lane == i1, -jnp.inf, logits)
    m2 = jnp.max(rest, axis=-1, keepdims=True)
    i2 = jnp.min(jnp.where(rest == m2, lane, LANES), axis=-1, keepdims=True)
    e = jnp.exp(m2 - m1)
    w1 = 1.0 / (1.0 + e)
    w2 = e * w1
    idx_ref[0] = jnp.where(lane == 0, i1, jnp.where(lane == 1, i2, 0))
    wt_ref[0] = jnp.where(lane == 0, w1, jnp.where(lane == 1, w2, 0.0))


def _norm_mod_router(h, gain, mods, router_w, router_b):
    b, s, d = h.shape
    ne = router_w.shape[1]
    rw = jnp.zeros((d, LANES), F32).at[:, :ne].set(router_w)
    rb = jnp.full((1, LANES), -1e30, F32).at[0, :ne].set(router_b)
    return pl.pallas_call(
        _norm_mod_router_kernel,
        grid=(b, s // ROW_TILE),
        in_specs=[pl.BlockSpec((1, ROW_TILE, d), lambda i, r: (i, r, 0)),
                  pl.BlockSpec((1, d), lambda i, r: (0, 0)),
                  pl.BlockSpec((1, 1, 2, d), lambda i, r: (i, 1, 0, 0)),
                  pl.BlockSpec((d, LANES), lambda i, r: (0, 0)),
                  pl.BlockSpec((1, LANES), lambda i, r: (0, 0))],
        out_specs=[pl.BlockSpec((1, ROW_TILE, d), lambda i, r: (i, r, 0)),
                   pl.BlockSpec((1, ROW_TILE, LANES), lambda i, r: (i, r, 0)),
                   pl.BlockSpec((1, ROW_TILE, LANES), lambda i, r: (i, r, 0))],
        out_shape=[jax.ShapeDtypeStruct((b, s, d), BF16),
                   jax.ShapeDtypeStruct((b, s, LANES), jnp.int32),
                   jax.ShapeDtypeStruct((b, s, LANES), F32)],
        compiler_params=_cparams(("parallel", "parallel")),
        name="norm_mod_router",
    )(h, gain.reshape(1, d), mods, rw, rb)


def _gelu_exact(x):
    return 0.5 * x * (1.0 + lax.erf(x * (2.0 ** -0.5)))


def _mm_kernel(*refs, nk, mode):
    n_in = {"swiglu": 3, "resid": 4}.get(mode, 2)
    a_ref, w_ref = refs[:2]
    o_ref = refs[n_in]
    if mode == "swiglu":
        w2_ref = refs[2]
    if mode == "resid":
        h_ref, g_ref = refs[2:4]
    if nk > 1:
        acc_ref = refs[n_in + 1]
        if mode == "swiglu":
            acc2_ref = refs[n_in + 2]
    k = pl.program_id(2)
    a = a_ref[...]
    p = jnp.dot(a, w_ref[...].astype(BF16), preferred_element_type=F32)
    if mode == "swiglu":
        p2 = jnp.dot(a, w2_ref[...].astype(BF16), preferred_element_type=F32)

    if nk > 1:
        @pl.when(k == 0)
        def _():
            acc_ref[...] = p
            if mode == "swiglu":
                acc2_ref[...] = p2

        @pl.when(k > 0)
        def _():
            acc_ref[...] += p
            if mode == "swiglu":
                acc2_ref[...] += p2

    def finish():
        r = acc_ref[...] if nk > 1 else p
        if mode == "plain":
            o_ref[...] = r.astype(o_ref.dtype)
        elif mode == "gelu":
            o_ref[...] = _gelu_exact(r).astype(o_ref.dtype)
        elif mode == "swiglu":
            r2 = acc2_ref[...] if nk > 1 else p2
            o_ref[...] = (_silu(r) * r2).astype(o_ref.dtype)
        else:
            o_ref[...] = h_ref[...] + g_ref[0] * r

    if nk > 1:
        pl.when(k == nk - 1)(finish)
    else:
        finish()


def _mm(a, w, *, mode, tm, tn, tk, out_dtype, n_out=None, w_col0=0, w2_col0=0, h=None, gate=None,
        rows_per_gate=None):
    m, kdim = a.shape
    n_out = w.shape[1] if n_out is None else n_out
    nk = kdim // tk
    assert m % tm == 0 and n_out % tn == 0 and kdim % tk == 0 and w_col0 % tn == 0 and w2_col0 % tn == 0
    cj, cj2 = w_col0 // tn, w2_col0 // tn
    in_specs = [pl.BlockSpec((tm, tk), lambda i, j, k: (i, k)),
                pl.BlockSpec((tk, tn), lambda i, j, k: (k, j + cj))]
    args = [a, w]
    n_acc = 0 if nk == 1 else (2 if mode == "swiglu" else 1)
    scratch = [pltpu.VMEM((tm, tn), F32)] * n_acc
    if mode == "swiglu":
        in_specs.append(pl.BlockSpec((tk, tn), lambda i, j, k: (k, j + cj2)))
        args.append(w)
    if mode == "resid":
        tiles_per_gate = rows_per_gate // tm
        in_specs += [pl.BlockSpec((tm, tn), lambda i, j, k: (i, j)),
                     pl.BlockSpec((1, 1, tn), lambda i, j, k: (i // tiles_per_gate, 0, j))]
        args += [h, gate]
    return pl.pallas_call(
        functools.partial(_mm_kernel, nk=nk, mode=mode),
        grid=(m // tm, n_out // tn, nk),
        in_specs=in_specs,
        out_specs=pl.BlockSpec((tm, tn), lambda i, j, k: (i, j)),
        out_shape=jax.ShapeDtypeStruct((m, n_out), out_dtype),
        scratch_shapes=scratch,
        compiler_params=_cparams(("parallel", "parallel", "arbitrary")),
        name="mm_" + mode,
    )(*args)


def _conv_kernel(x_ref, w_ref, o_ref, xp_ref, *, n_ctx, n_q_blocks, n_k_blocks):
    s = x_ref.shape[1]
    half = GDN_CONV // 2
    w = [w_ref[i:i + 1, :] for i in range(GDN_CONV)]
    cb = pl.program_id(1)
    q_scale = GDN_HEAD_DIM ** -0.5
    pad = CONV_PAD_ROWS
    zeros = jnp.zeros((pad, LANES), F32)
    base = pad
    for lo, hi in ((0, n_ctx), (n_ctx, s)):
        n = hi - lo
        xp_ref[base - pad:base, :] = zeros
        xp_ref[base:base + n, :] = x_ref[0, lo:hi, :].astype(F32)
        xp_ref[base + n:base + n + pad, :] = zeros
        y = xp_ref[base:base + n, :] * w[half]
        for o in range(-half, half + 1):
            if o != 0:
                y = y + xp_ref[base + o:base + o + n, :] * w[o + half]
        base += n + pad
        y = _silu(y)
        fac = lax.rsqrt(jnp.sum(y * y, axis=-1, keepdims=True) + EPS)
        scale = jnp.where(cb < n_q_blocks, fac * q_scale, jnp.where(cb < n_q_blocks + n_k_blocks, fac, 1.0))
        o_ref[0, lo:hi, :] = (y * scale).astype(o_ref.dtype)


def _gdn_conv(proj, conv_w, n_ctx, key_dim):
    b, s, _ = proj.shape
    c = conv_w.shape[1]
    nqb = key_dim // LANES
    return pl.pallas_call(
        functools.partial(_conv_kernel, n_ctx=n_ctx, n_q_blocks=nqb, n_k_blocks=nqb),
        grid=(b, c // LANES),
        in_specs=[pl.BlockSpec((1, s, LANES), lambda i, j: (i, 0, j)),
                  pl.BlockSpec((GDN_CONV, LANES), lambda i, j: (0, j))],
        out_specs=pl.BlockSpec((1, s, LANES), lambda i, j: (i, 0, j)),
        out_shape=jax.ShapeDtypeStruct((b, s, c), BF16),
        scratch_shapes=[pltpu.VMEM((s + 3 * CONV_PAD_ROWS, LANES), F32)],
        compiler_params=_cparams(("parallel", "parallel")),
        name="gdn_conv",
    )(proj, conv_w)


def _gates_kernel(ba_ref, alog_ref, dtb_ref, o_ref, *, n_dir_lanes):
    x = ba_ref[0]
    lc = lax.broadcasted_iota(jnp.int32, (GDN_CHUNK, LANES), 1)
    half = 2 * n_dir_lanes
    beta = 1.0 / (1.0 + jnp.exp(-x))
    z = x + dtb_ref[...]
    softplus = jnp.maximum(z, 0.0) + jnp.log1p(jnp.exp(-jnp.abs(z)))
    g = -jnp.exp(alog_ref[...]) * softplus
    ri = lax.broadcasted_iota(jnp.int32, (GDN_CHUNK, GDN_CHUNK), 0)
    ci = lax.broadcasted_iota(jnp.int32, (GDN_CHUNK, GDN_CHUNK), 1)
    lower = (ri >= ci).astype(F32)
    upper = (ri <= ci).astype(F32)
    for c in range(x.shape[0] // GDN_CHUNK):
        rows = slice(c * GDN_CHUNK, (c + 1) * GDN_CHUNK)
        gc = g[rows]
        fwd = jnp.dot(lower, gc, preferred_element_type=F32, precision=HIGHEST)
        bwd = jnp.dot(upper, gc, preferred_element_type=F32, precision=HIGHEST)
        gcs =jnp.where(lc < half + n_dir_lanes, fwd, bwd)
        o_ref[0, rows, :] = jnp.where(lc < half, beta[rows], gcs)


def _gdn_gates(ba, a_log, dt_bias):
    b, s, _ = ba.shape
    nd = a_log.shape[-1]
    zeros = jnp.zeros((2 * nd,), F32)
    alog = jnp.concatenate([zeros, a_log.reshape(-1)]).reshape(1, LANES)
    dtb = jnp.concatenate([zeros, dt_bias.reshape(-1)]).reshape(1, LANES)
    return pl.pallas_call(
        functools.partial(_gates_kernel, n_dir_lanes=nd),
        grid=(b, s // GDN_STEP),
        in_specs=[pl.BlockSpec((1, GDN_STEP, LANES), lambda i, r: (i, r, 0)),
                  pl.BlockSpec((1, LANES), lambda i, r: (0, 0)),
                  pl.BlockSpec((1, LANES), lambda i, r: (0, 0))],
        out_specs=pl.BlockSpec((1, GDN_STEP, LANES), lambda i, r: (i, r, 0)),
        out_shape=jax.ShapeDtypeStruct((b, s, LANES), F32),
        compiler_params=_cparams(("parallel", "parallel")),
        name="gdn_gates",
    )(ba, alog, dtb)


def _gdn_kernel(q_ref, k_ref, v_ref, col_ref, o_ref, s_ref, *, direction, n_dir_lanes):
    hg = pl.program_id(1)
    step = pl.program_id(2)
    hq = GDN_QK_HEADS_PER_STEP
    nv = 2 * hq
    c = GDN_CHUNK
    dh = GDN_HEAD_DIM
    n_chunks = GDN_STEP // c

    @pl.when(step == 0)
    def _():
        s_ref[...] = jnp.zeros_like(s_ref)

    col = col_ref[0]
    lane0 = direction * n_dir_lanes + hg * nv
    beta_g = pltpu.roll(col, (LANES - lane0) % LANES, 1)
    gcs_g = pltpu.roll(col, (LANES // 2 - lane0) % LANES, 1)
    gcs_r = gcs_g.T

    ri = lax.broadcasted_iota(jnp.int32, (c, c), 0)
    ci = lax.broadcasted_iota(jnp.int32, (c, c), 1)
    if direction == 0:
        incl, strict = ri >= ci, ri > ci
    else:
        incl, strict = ri <= ci, ri < ci
    eye = (ri == ci).astype(F32)
    blockdiag = (ri // 16) == (ci // 16)
    order = range(n_chunks) if direction == 0 else range(n_chunks - 1, -1, -1)
    heads = range(nv)
    order = list(order)
    groups = [order[i:i + GDN_CHUNKS_PER_STAGE] for i in range(0, n_chunks, GDN_CHUNKS_PER_STAGE)]
    for grp in groups:
        rows = {p: slice(p * c, (p + 1) * c) for p in grp}
        pq = [(p, qh) for p in grp for qh in range(hq)]
        pj = [(p, j) for p in grp for j in heads]
        ks = {(p, qh): k_ref[0, rows[p], qh * dh:(qh + 1) * dh] for p, qh in pq}
        qs = {(p, qh): q_ref[0, rows[p], qh * dh:(qh + 1) * dh] for p, qh in pq}
        kk = {x: _dot_nt(ks[x], ks[x]) for x in pq}
        qk = {x: _dot_nt(qs[x], ks[x]) for x in pq}
        k_t = {x: ks[x].astype(F32).T for x in pq}
        bcol = {(p, j): beta_g[rows[p], j:j + 1] for p, j in pj}
        gcol = {(p, j): gcs_g[rows[p], j:j + 1] for p, j in pj}
        grow = {(p, j): gcs_r[j:j + 1, rows[p]] for p, j in pj}
        glast = {x: (g[:, c - 1:c] if direction == 0 else g[:, 0:1]) for x, g in grow.items()}
        decay = {x: jnp.exp(jnp.where(incl, gcol[x] - grow[x], -jnp.inf)) for x in pj}
        a = {(p, j): jnp.where(strict, kk[p, j // 2] * bcol[p, j] * decay[p, j], 0.0) for p, j in pj}
        a_d = {x: jnp.where(blockdiag, a[x], 0.0) for x in pj}
        a_o = {x: a[x] - a_d[x] for x in pj}
        p2 = {x: _dot(a_d[x], a_d[x]) for x in pj}
        p4 = {x: _dot(p2[x], p2[x]) for x in pj}
        p8 = {x: _dot(p4[x], p4[x]) for x in pj}
        t1 = {x: _dot(eye - a_d[x], eye + p2[x]) for x in pj}
        t2 = {x: _dot(t1[x], eye + p4[x]) for x in pj}
        t_d = {x: _dot(t2[x], eye + p8[x]) for x in pj}
        nn = {x: _dot(t_d[x], a_o[x]) for x in pj}
        n2 = {x: _dot(nn[x], nn[x]) for x in pj}
        rr = {x: _dot(eye - nn[x], eye + n2[x]) for x in pj}
        t = {x: _dot(rr[x], t_d[x]) for x in pj}
        rhs = {(p, j): jnp.concatenate([v_ref[0, rows[p], j * dh:(j + 1) * dh].astype(F32) * bcol[p, j],
                                        ks[p, j // 2].astype(F32) * (bcol[p, j] * jnp.exp(gcol[p, j]))], axis=1)
               for p, j in pj}
        uw = {x: _dot(t[x], rhs[x]) for x in pj}
        wq_lhs = {(p, j): jnp.concatenate([uw[p, j][:, dh:], qs[p, j // 2].astype(F32) * jnp.exp(gcol[p, j])], axis=0)
                  for p, j in pj}
        lhs2 = {(p, j): jnp.concatenate([jnp.where(incl, qk[p, j // 2] * decay[p, j], 0.0),
                                         k_t[p, j // 2] * jnp.exp(glast[p, j] - grow[p, j])], axis=0)
                for p, j in pj}
        for p in grp:
            state = [s_ref[j] for j in heads]
            wq = [_dot(wq_lhs[p, j], state[j]) for j in heads]
            v_new = [uw[p, j][:, :dh] - wq[j][:c] for j in heads]
            r2 = [_dot(lhs2[p, j], v_new[j]) for j in heads]
            for j in heads:
                o_ref[0, rows[p], j * dh:(j + 1) * dh] = (wq[j][c:] + r2[j][:c]).astype(o_ref.dtype)
                s_ref[j] = state[j] * jnp.exp(glast[p, j]) + r2[j][c:]


def _gdn_scan(qkv, col, direction, n_ctx, key_dim, n_v_heads):
    b, s, _ = qkv.shape
    hq = GDN_QK_HEADS_PER_STEP
    nv = 2 * hq
    n_steps = s // GDN_STEP
    n_ctx_steps = n_ctx // GDN_STEP
    assert n_ctx_steps == 1
    qw = hq * GDN_HEAD_DIM
    vw = nv * GDN_HEAD_DIM
    n_groups = key_dim // qw
    k_blk0 = key_dim // qw
    v_blk0 = 2 * key_dim // vw

    def blk(st):
        if direction == 0:
            return st
        return jnp.where(st == 0, 0, n_steps - st)

    def oblk(st):
        return blk(jnp.maximum(st, 1)) - 1

    return pl.pallas_call(
        functools.partial(_gdn_kernel, direction=direction, n_dir_lanes=n_v_heads),
        grid=(b, n_groups, n_steps),
        in_specs=[pl.BlockSpec((1, GDN_STEP, qw), lambda i, g, st: (i, blk(st), g)),
                  pl.BlockSpec((1, GDN_STEP, qw), lambda i, g, st: (i, blk(st), k_blk0 + g)),
                  pl.BlockSpec((1, GDN_STEP, vw), lambda i, g, st: (i, blk(st), v_blk0 + g)),
                  pl.BlockSpec((1, GDN_STEP, LANES), lambda i, g, st: (i, blk(st), 0))],
        out_specs=pl.BlockSpec((1, GDN_STEP, vw), lambda i, g, st: (i, oblk(st), g)),
        out_shape=jax.ShapeDtypeStruct((b, s - n_ctx, n_v_heads * GDN_HEAD_DIM), BF16),
        scratch_shapes=[pltpu.VMEM((nv, GDN_HEAD_DIM, GDN_HEAD_DIM), F32)],
        compiler_params=_cparams(("parallel", "parallel", "arbitrary")),
        name="gdn_scan_d%d" % direction,
    )(qkv, qkv, qkv, col)


def _gdn_out_kernel(o0_ref, o1_ref, z_ref, g_ref, y_ref):
    dh = GDN_HEAD_DIM
    for hd in range(o0_ref.shape[2] // dh):
        sl = slice(hd * dh, (hd + 1) * dh)
        o = o0_ref[0, :, sl].astype(F32) + o1_ref[0, :, sl].astype(F32)
        y = o * lax.rsqrt(jnp.mean(o * o, axis=-1, keepdims=True) + EPS) * g_ref[...]
        y_ref[0, :, sl] = (y * _silu(z_ref[0, :, sl].astype(F32))).astype(y_ref.dtype)


def _gdn_out_norm(o0, o1, proj, o_norm, n_ctx, z_col0):
    b, l, w = o0.shape
    ctx_tiles = n_ctx // ROW_TILE
    zb = z_col0 // w
    return pl.pallas_call(
        _gdn_out_kernel,
        grid=(b, l // ROW_TILE),
        in_specs=[pl.BlockSpec((1, ROW_TILE, w), lambda i, r: (i, r, 0)),
                  pl.BlockSpec((1, ROW_TILE, w), lambda i, r: (i, r, 0)),
                  pl.BlockSpec((1, ROW_TILE, w), lambda i, r: (i, r + ctx_tiles, zb)),
                  pl.BlockSpec((1, GDN_HEAD_DIM), lambda i, r: (0, 0))],
        out_specs=pl.BlockSpec((1, ROW_TILE, w), lambda i, r: (i, r, 0)),
        out_shape=jax.ShapeDtypeStruct((b, l, w), BF16),
        compiler_params=_cparams(("parallel", "parallel")),
        name="gdn_out_norm",
    )(o0, o1, proj, o_norm.reshape(1, -1))


def _cm_kernel(gu_ref, gv_ref, lg_ref, lb_ref, ws_ref, bs_ref, o_ref):
    gd = gu_ref.shape[1] // CM_GROUPS
    for ch in range(gu_ref.shape[0] // CM_CHUNK):
        rows = slice(ch * CM_CHUNK, (ch + 1) * CM_CHUNK)
        gv = gv_ref[rows, :].astype(F32)
        mu = jnp.mean(gv, axis=-1, keepdims=True)
        xc = gv - mu
        var = jnp.mean(xc * xc, axis=-1, keepdims=True)
        gvn = xc * lax.rsqrt(var + EPS) * lg_ref[...] + lb_ref[...]
        for g in range(CM_GROUPS):
            sl = slice(g * gd, (g + 1) * gd)
            mixed = _dot(ws_ref[g], gvn[:, sl]) + bs_ref[:, g:g + 1]
            o_ref[rows, sl] = (gu_ref[rows, sl].astype(F32) * mixed).astype(o_ref.dtype)


def _cm_spatial(guv, ln_g, ln_b, w_s, b_s):
    t, w2 = guv.shape
    w = w2 // 2
    rt = 2 * CM_CHUNK
    return pl.pallas_call(
        _cm_kernel,
        grid=(t // rt,),
        in_specs=[pl.BlockSpec((rt, w), lambda i: (i, 0)),
                  pl.BlockSpec((rt, w), lambda i: (i, 1)),
                  pl.BlockSpec((1, w), lambda i: (0, 0)),
                  pl.BlockSpec((1, w), lambda i: (0, 0)),
                  pl.BlockSpec((CM_GROUPS, CM_CHUNK, CM_CHUNK), lambda i: (0, 0, 0)),
                  pl.BlockSpec((CM_CHUNK, CM_GROUPS), lambda i: (0, 0))],
        out_specs=pl.BlockSpec((rt, w), lambda i: (i, 0)),
        out_shape=jax.ShapeDtypeStruct((t, w), BF16),
        compiler_params=_cparams(("parallel",)),
        name="cm_spatial",
    )(guv, guv, ln_g.reshape(1, w), ln_b.reshape(1, w), w_s, jnp.transpose(b_s))


def _moe_plan(idx2):
    t = idx2.shape[0]
    a = t * TOP_K
    e = idx2.reshape(a)
    onehot = (e[:, None] == jnp.arange(N_EXPERTS, dtype=jnp.int32)[None, :]).astype(jnp.int32)
    rank = jnp.sum((jnp.cumsum(onehot, axis=0) - onehot) * onehot, axis=1)
    counts = jnp.sum(onehot, axis=0)
    padded = ((counts + MOE_TM - 1) // MOE_TM) * MOE_TM
    ends = jnp.cumsum(padded)
    starts = ends - padded
    pos = starts[e] + rank
    a_pad = a + N_EXPERTS * MOE_TM
    row_token = jnp.zeros((a_pad,), jnp.int32).at[pos].set(jnp.arange(a, dtype=jnp.int32) // TOP_K,
                                                           unique_indices=True)
    n_tiles = a_pad // MOE_TM
    tile_start = jnp.arange(n_tiles, dtype=jnp.int32) * MOE_TM
    tile_expert = jnp.minimum(jnp.searchsorted(ends, tile_start, side="right"), N_EXPERTS - 1).astype(jnp.int32)
    tile_rows = jnp.clip((starts + counts)[tile_expert] - tile_start, 0, MOE_TM).astype(jnp.int32)
    return pos.reshape(t, TOP_K), row_token, tile_expert, tile_rows


def _gather_kernel(idx_ref, src_ref, dst_ref, sem):
    def issue(r, carry):
        pltpu.make_async_copy(src_ref.at[idx_ref[0, 0, r]], dst_ref.at[r], sem).start()
        return carry

    lax.fori_loop(0, GATHER_ROWS, issue, 0, unroll=DMA_UNROLL)

    def drain(r, carry):
        pltpu.make_async_copy(src_ref.at[0], dst_ref.at[0], sem).wait()
        return carry

    lax.fori_loop(0, GATHER_ROWS, drain, 0, unroll=DMA_UNROLL)


def _gather_rows(src3, row_idx):
    r = row_idx.shape[0]
    nb = r // GATHER_ROWS
    return pl.pallas_call(
        _gather_kernel,
        grid=(nb,),
        in_specs=[pl.BlockSpec((1, 1, GATHER_ROWS), lambda i: (i, 0, 0), memory_space=pltpu.SMEM),
                  pl.BlockSpec(memory_space=pl.ANY)],
        out_specs=pl.BlockSpec((GATHER_ROWS,) + src3.shape[1:], lambda i: (i, 0, 0)),
        out_shape=jax.ShapeDtypeStruct((r,) + src3.shape[1:], src3.dtype),
        scratch_shapes=[pltpu.SemaphoreType.DMA(())],
        compiler_params=_cparams(("arbitrary",)),
        name="gather_rows",
    )(row_idx.reshape(nb, 1, GATHER_ROWS), src3)


def _moe_gu_kernel(te_ref, tr_ref, x2_ref, wg_ref, wu_ref, o_ref, wb_ref):
    rows = tr_ref[pl.program_id(0)]
    tm = o_ref.shape[0]

    def swiglu(x, wg, wu):
        g = jnp.dot(x, wg, preferred_element_type=F32)
        u = jnp.dot(x, wu, preferred_element_type=F32)
        return (_silu(g) * u).astype(o_ref.dtype)

    @pl.when(rows == tm)
    def _():
        o_ref[...] = swiglu(x2_ref[...], wg_ref[0].astype(BF16), wu_ref[0].astype(BF16))

    @pl.when((rows > 0) & (rows < tm))
    def _():
        wb_ref[0] = wg_ref[0].astype(BF16)
        wb_ref[1] = wu_ref[0].astype(BF16)
        for sb in range(tm // MOE_SUB):
            sl = slice(sb * MOE_SUB, (sb + 1) * MOE_SUB)

            @pl.when(sb * MOE_SUB < rows)
            def _():
                o_ref[sl, :] = swiglu(x2_ref[sl, :], wb_ref[0], wb_ref[1])

            @pl.when(sb * MOE_SUB >= rows)
            def _():
                o_ref[sl, :] = jnp.zeros((MOE_SUB, o_ref.shape[1]), o_ref.dtype)

    @pl.when(rows == 0)
    def _():
        o_ref[...] = jnp.zeros_like(o_ref)


def _moe_gu(xs, w_gu, tile_expert, tile_rows, tn=512):
    a_pad, d = xs.shape
    f = w_gu.shape[2] // 2
    nj = f // tn
    return pl.pallas_call(
        _moe_gu_kernel,
        grid_spec=pltpu.PrefetchScalarGridSpec(
            num_scalar_prefetch=2,
            grid=(a_pad // MOE_TM, nj),
            in_specs=[pl.BlockSpec((MOE_TM, d), lambda i, j, te, tr: (i, 0)),
                      pl.BlockSpec((1, d, tn), lambda i, j, te, tr: (te[i], 0, j)),
                      pl.BlockSpec((1, d, tn), lambda i, j, te, tr: (te[i], 0, j + nj))],
            out_specs=pl.BlockSpec((MOE_TM, tn), lambda i, j, te, tr: (i, j)),
            scratch_shapes=[pltpu.VMEM((2, d, tn), BF16)]),
        out_shape=jax.ShapeDtypeStruct((a_pad, f), BF16),
        compiler_params=_cparams(("parallel", "parallel")),
        name="moe_gu",
    )(tile_expert, tile_rows, xs, w_gu, w_gu)


def _moe_down_kernel(te_ref, tr_ref, a_ref, w_ref, o_ref, acc_ref, wb_ref, *, nk):
    k = pl.program_id(2)
    rows = tr_ref[pl.program_id(0)]
    tm = acc_ref.shape[0]

    @pl.when(rows == tm)
    def _():
        p = jnp.dot(a_ref[...], w_ref[0].astype(BF16), preferred_element_type=F32)

        @pl.when(k == 0)
        def _():
            acc_ref[...] = p

        @pl.when(k > 0)
        def _():
            acc_ref[...] += p

    @pl.when((rows > 0) & (rows < tm))
    def _():
        @pl.when(k == 0)
        def _():
            acc_ref[...] = jnp.zeros_like(acc_ref)

        wb_ref[...] = w_ref[0].astype(BF16)
        for sb in range(tm // MOE_SUB):
            sl = slice(sb * MOE_SUB, (sb + 1) * MOE_SUB)

            @pl.when(sb * MOE_SUB < rows)
            def _():
                acc_ref[sl, :] += jnp.dot(a_ref[sl, :], wb_ref[...], preferred_element_type=F32)

    @pl.when((rows > 0) & (k == nk - 1))
    def _():
        o_ref[...] = acc_ref[...].astype(o_ref.dtype)

    @pl.when((rows == 0) & (k == nk - 1))
    def _():
        o_ref[...] = jnp.zeros_like(o_ref)


def _moe_down(hmid, w_down, tile_expert, tile_rows, tn=1024, tk=1792):
    a_pad, f = hmid.shape
    d = w_down.shape[2]
    nk = f // tk
    return pl.pallas_call(
        functools.partial(_moe_down_kernel, nk=nk),
        grid_spec=pltpu.PrefetchScalarGridSpec(
            num_scalar_prefetch=2,
            grid=(a_pad // MOE_TM, d // tn, nk),
            in_specs=[pl.BlockSpec((MOE_TM, tk), lambda i, j, k, te, tr: (i, k)),
                      pl.BlockSpec((1, tk, tn), lambda i, j, k, te, tr: (te[i], k, j))],
            out_specs=pl.BlockSpec((MOE_TM, tn), lambda i, j, k, te, tr: (i, j)),
            scratch_shapes=[pltpu.VMEM((MOE_TM, tn), F32), pltpu.VMEM((tk, tn), BF16)]),
        out_shape=jax.ShapeDtypeStruct((a_pad, d), BF16),
        compiler_params=_cparams(("parallel", "parallel", "arbitrary")),
        name="moe_down",
    )(tile_expert, tile_rows, hmid, w_down)


def _combine_kernel(pos_ref, y_ref, wt_ref, h_ref, gt_ref, gn_ref, o_ref, buf_ref, sem):
    def issue(r, carry):
        pltpu.make_async_copy(y_ref.at[pos_ref[0, 0, r]], buf_ref.at[r], sem).start()
        return carry

    n = buf_ref.shape[0]
    lax.fori_loop(0, n, issue, 0, unroll=DMA_UNROLL)

    def drain(r, carry):
        pltpu.make_async_copy(y_ref.at[0], buf_ref.at[0], sem).wait()
        return carry

    lax.fori_loop(0, n, drain, 0, unroll=DMA_UNROLL)
    tt = o_ref.shape[0]
    w0 = wt_ref[:, 0:1]
    w1 = wt_ref[:, 1:2]
    ss = jnp.zeros((tt, 1), F32)
    for s in range(buf_ref.shape[1]):
        sl = slice(s * LANES, (s + 1) * LANES)
        y = w0 * buf_ref[pl.ds(0, tt), s, :].astype(F32) + w1 * buf_ref[pl.ds(tt, tt), s, :].astype(F32)
        hn = h_ref[:, sl] + gt_ref[0, :, sl] * y
        ss = ss + jnp.sum(hn * hn, axis=-1, keepdims=True)
        o_ref[:, sl] = hn
    o_ref[...] = o_ref[...] * lax.rsqrt(ss * (1.0 / o_ref.shape[1]) + EPS) * gn_ref[...]


def _moe_combine_final(y3, pos_tiles, wt, h, gate, gain, tokens_per_batch, tt=256):
    t, d = h.shape
    sd = d // LANES
    tiles_per_batch = tokens_per_batch // tt
    return pl.pallas_call(
        _combine_kernel,
        grid=(t // tt,),
        in_specs=[pl.BlockSpec((1, 1, TOP_K * tt), lambda i: (i, 0, 0), memory_space=pltpu.SMEM),
                  pl.BlockSpec(memory_space=pl.ANY),
                  pl.BlockSpec((tt, LANES), lambda i: (i, 0)),
                  pl.BlockSpec((tt, d), lambda i: (i, 0)),
                  pl.BlockSpec((1, 1, d), lambda i: (i // tiles_per_batch, 0, 0)),
                  pl.BlockSpec((1, d), lambda i: (0, 0))],
        out_specs=pl.BlockSpec((tt, d), lambda i: (i, 0)),
        out_shape=jax.ShapeDtypeStruct((t, d), F32),
        scratch_shapes=[pltpu.VMEM((TOP_K * tt, sd, LANES), y3.dtype), pltpu.SemaphoreType.DMA(())],
        compiler_params=_cparams(("arbitrary",)),
        name="moe_combine",
    )(pos_tiles, y3, wt, h, gate, gain.reshape(1, d))


def kernel(x, c, ctx, c_ctx, ada_w, ada_b, norm_mix, norm_ffn, norm_final, gdn_w_in, gdn_conv, gdn_a_log,
           gdn_dt_bias, gdn_o_norm, gdn_w_out, cm_w_in, cm_ln_g, cm_ln_b, cm_w_s, cm_b_s, cm_w_out, ffn_w_gu,
           ffn_w_down, moe_router, moe_router_b, moe_w_gu, moe_w_down):
    b, l, d = x.shape
    n_ctx = ctx.shape[1]
    s = n_ctx + l
    t = b * l
    n_v_heads = gdn_a_log.shape[-1]
    val_dim = n_v_heads * GDN_HEAD_DIM
    key_dim = (gdn_conv.shape[-1] - val_dim) // 2
    conv_dim = 2 * key_dim + val_dim
    ffn_dim = ffn_w_down.shape[1]

    cond8 = jnp.zeros((8, d), F32).at[:b].set(c).at[b].set(c_ctx)
    mods = _ada(cond8, ada_w, ada_b).reshape(ada_w.shape[0], 8, 6, d)

    def seg_mods(layer, first):
        lat = mods[layer, :b, first:first + 2]
        cx = jnp.broadcast_to(mods[layer, b, first:first + 2][None], lat.shape)
        return jnp.stack([cx, lat], axis=1)

    hcat = jnp.concatenate([ctx, x], axis=1)
    u = _norm_mod(hcat, norm_mix[0], seg_mods(0, 0), n_ctx // ROW_TILE).reshape(b * s, d)
    w_in = gdn_w_in[0]
    n_qkvz = conv_dim + val_dim
    tm_cat = _tile(b * s, MM_TM)
    tm = _tile(l, MM_TM)
    proj = _mm(u, w_in, mode="plain", tm=tm_cat, tn=1024, tk=d, out_dtype=BF16, n_out=n_qkvz).reshape(b, s, n_qkvz)
    ba = _mm(u, w_in, mode="plain", tm=tm_cat, tn=LANES, tk=d, out_dtype=F32, n_out=LANES, w_col0=n_qkvz)
    qkv = _gdn_conv(proj, gdn_conv[0], n_ctx, key_dim)
    col = _gdn_gates(ba.reshape(b, s, LANES), gdn_a_log[0], gdn_dt_bias[0])
    o_fwd = _gdn_scan(qkv, col, 0, n_ctx, key_dim, n_v_heads)
    o_bwd = _gdn_scan(qkv, col, 1, n_ctx, key_dim, n_v_heads)
    y = _gdn_out_norm(o_fwd, o_bwd, proj, gdn_o_norm[0], n_ctx, conv_dim).reshape(t, val_dim)
    gate = lambda layer, idx: mods[layer, :b, idx].reshape(b, 1, d)
    h = _mm(y, gdn_w_out[0], mode="resid", tm=tm,tn=512, tk=val_dim, out_dtype=F32, h=x.reshape(t, d),
            gate=gate(0, 2), rows_per_gate=l)

    tf = _norm_mod(h.reshape(b, l, d), norm_ffn[0], seg_mods(0, 3), 0).reshape(t, d)
    mid = _mm(tf, ffn_w_gu[0], mode="swiglu", tm=tm,tn=512, tk=d, out_dtype=BF16, n_out=ffn_dim, w2_col0=ffn_dim)
    h = _mm(mid, ffn_w_down[0], mode="resid", tm=tm,tn=256, tk=ffn_dim, out_dtype=F32, h=h, gate=gate(0, 5),
            rows_per_gate=l)

    u = _norm_mod(h.reshape(b, l, d), norm_mix[1], seg_mods(1, 0), 0).reshape(t, d)
    guv = _mm(u, cm_w_in[0], mode="gelu", tm=tm, tn=1024, tk=d, out_dtype=BF16)
    cmix = _cm_spatial(guv, cm_ln_g[0], cm_ln_b[0], cm_w_s[0], cm_b_s[0])
    h = _mm(cmix, cm_w_out[0], mode="resid", tm=tm,tn=512, tk=cmix.shape[1], out_dtype=F32, h=h, gate=gate(1, 2),
            rows_per_gate=l)

    tf, idx, wt = _norm_mod_router(h.reshape(b, l, d), norm_ffn[1], seg_mods(1, 3), moe_router[0], moe_router_b[0])
    pos, row_token, tile_expert, tile_rows = _moe_plan(idx.reshape(t, LANES)[:, :TOP_K])
    sd = d // LANES
    xs = _gather_rows(tf.reshape(t, sd, LANES), row_token).reshape(-1, d)
    hmid = _moe_gu(xs, moe_w_gu[0], tile_expert, tile_rows)
    ys3 = _moe_down(hmid, moe_w_down[0], tile_expert, tile_rows).reshape(-1, sd, LANES)
    tt = 256
    pos_tiles = jnp.transpose(pos.reshape(t // tt, tt, TOP_K), (0, 2, 1)).reshape(t // tt, 1, TOP_K * tt)
    out = _moe_combine_final(ys3, pos_tiles, wt.reshape(t, LANES), h, gate(1, 5), norm_final, l, tt)
    return out.reshape(b, l, d)
```

```python
import functools

import jax
import jax.numpy as jnp
from jax import lax
from jax.experimental import pallas as pl
from jax.experimental.pallas import tpu as pltpu

F32 = jnp.float32
BF16 = jnp.bfloat16
EPS = 1e-6
HIGHEST = lax.Precision.HIGHEST

LANES = 128
VMEM_LIMIT_BYTES = 56 * 1024 * 1024

GDN_HEAD_DIM = 128
GDN_CHUNK = 64
GDN_CONV = 5
CONV_PAD_ROWS = 8
CM_CHUNK = 128
CM_GROUPS = 16
N_EXPERTS = 8
TOP_K = 2

ROW_TILE = 256
GDN_STEP = 256
GDN_QK_HEADS_PER_STEP = 8
GDN_CHUNKS_PER_STAGE = 4
MM_TM = 1024
MOE_TM = 1024
MOE_SUB = 256
GATHER_ROWS = 512
DMA_UNROLL = 8


def _cparams(sem):
    return pltpu.CompilerParams(dimension_semantics=sem, vmem_limit_bytes=VMEM_LIMIT_BYTES)


def _tile(n, preferred):
    t = preferred
    while n % t:
        t //= 2
    return t


def _silu(x):
    return x * (1.0 / (1.0 + jnp.exp(-x)))


def _dot(a, b):
    return jnp.dot(a.astype(BF16), b.astype(BF16), preferred_element_type=F32)


def _dotb(a, b):
    return jnp.dot(a, b, preferred_element_type=F32).astype(BF16)


def _dot_nt(a, b):
    return lax.dot_general(a.astype(BF16), b.astype(BF16), (((1,), (1,)), ((), ())),
                           preferred_element_type=F32)


def _ada_kernel(c_ref, w_ref, b_ref, o_ref):
    s = _silu(c_ref[...])
    o_ref[0] = jnp.dot(s, w_ref[0], preferred_element_type=F32, precision=HIGHEST) + b_ref[0]


def _ada(cond8, ada_w, ada_b, tn=1024):
    depth, d, n = ada_w.shape
    return pl.pallas_call(
        _ada_kernel,
        grid=(depth, n // tn),
        in_specs=[pl.BlockSpec((8, d), lambda l, j: (0, 0)),
                  pl.BlockSpec((1, d, tn), lambda l, j: (l, 0, j)),
                  pl.BlockSpec((1, 1, tn), lambda l, j: (l, 0, j))],
        out_specs=pl.BlockSpec((1, 8, tn), lambda l, j: (l, 0, j)),
        out_shape=jax.ShapeDtypeStruct((depth, 8, n), F32),
        compiler_params=_cparams(("parallel", "parallel")),
        name="ada",
    )(cond8, ada_w, ada_b.reshape(depth, 1, n))


def _norm_mod_kernel(*refs, n_ctx_tiles):
    if n_ctx_tiles:
        c_ref, h_ref, g_ref, m_ref, o_ref = refs
        x = jnp.where(pl.program_id(1) < n_ctx_tiles, c_ref[0], h_ref[0])
    else:
        h_ref, g_ref, m_ref, o_ref = refs
        x = h_ref[0]
    y = x * lax.rsqrt(jnp.mean(x * x, axis=-1, keepdims=True) + EPS) * g_ref[...]
    o_ref[0] = (y * (1.0 + m_ref[0, 0, 1:2, :]) + m_ref[0, 0, 0:1, :]).astype(o_ref.dtype)


def _norm_mod(h, gain, mods, ctx=None):
    b, l, d = h.shape
    nct = 0 if ctx is None else ctx.shape[1] // ROW_TILE
    in_specs = [pl.BlockSpec((1, ROW_TILE, d), lambda i, r: (i, jnp.maximum(r - nct, 0), 0)),
                pl.BlockSpec((1, d), lambda i, r: (0, 0)),
                pl.BlockSpec((1, 1, 2, d), lambda i, r: (i, jnp.where(r < nct, 0, 1), 0, 0))]
    args = [h, gain.reshape(1, d), mods]
    if nct:
        in_specs.insert(0, pl.BlockSpec((1, ROW_TILE, d), lambda i, r: (i, jnp.minimum(r, nct - 1), 0)))
        args.insert(0, ctx)
    return pl.pallas_call(
        functools.partial(_norm_mod_kernel, n_ctx_tiles=nct),
        grid=(b, nct + l // ROW_TILE),
        in_specs=in_specs,
        out_specs=pl.BlockSpec((1, ROW_TILE, d), lambda i, r: (i, r, 0)),
        out_shape=jax.ShapeDtypeStruct((b, nct * ROW_TILE + l, d), BF16),
        compiler_params=_cparams(("parallel", "parallel")),
        name="norm_mod",
    )(*args)


def _norm_mod_router_kernel(h_ref, g_ref, m_ref, rw_ref, rb_ref, o_ref, idx_ref, wt_ref):
    x = h_ref[0]
    y = x * lax.rsqrt(jnp.mean(x * x, axis=-1, keepdims=True) + EPS) * g_ref[...]
    t = y * (1.0 + m_ref[0, 0, 1:2, :]) + m_ref[0, 0, 0:1, :]
    o_ref[0] = t.astype(o_ref.dtype)
    logits = jnp.dot(t, rw_ref[...], preferred_element_type=F32, precision=HIGHEST) + rb_ref[...]
    lane = lax.broadcasted_iota(jnp.int32, logits.shape, 1)
    m1 = jnp.max(logits, axis=-1, keepdims=True)
    i1 = jnp.min(jnp.where(logits == m1, lane, LANES), axis=-1, keepdims=True)
    rest = jnp.where(lane == i1, -jnp.inf, logits)
    m2 = jnp.max(rest, axis=-1, keepdims=True)
    i2 = jnp.min(jnp.where(rest == m2, lane, LANES), axis=-1, keepdims=True)
    e = jnp.exp(m2 - m1)
    w1 = 1.0 / (1.0 + e)
    w2 = e * w1
    idx_ref[0] = jnp.where(lane == 0, i1, jnp.where(lane == 1, i2, 0))
    wt_ref[0] = jnp.where(lane == 0, w1, jnp.where(lane == 1, w2, 0.0))


def _norm_mod_router(h, gain, mods, router_w, router_b):
    b, s, d = h.shape
    ne = router_w.shape[1]
    rw = jnp.zeros((d, LANES), F32).at[:, :ne].set(router_w)
    rb = jnp.full((1, LANES), -1e30, F32).at[0, :ne].set(router_b)
    return pl.pallas_call(
        _norm_mod_router_kernel,
        grid=(b, s // ROW_TILE),
        in_specs=[pl.BlockSpec((1, ROW_TILE, d), lambda i, r: (i, r, 0)),
                  pl.BlockSpec((1, d), lambda i, r: (0, 0)),
                  pl.BlockSpec((1, 1, 2, d), lambda i, r: (i, 1, 0, 0)),
                  pl.BlockSpec((d, LANES), lambda i, r: (0, 0)),
                  pl.BlockSpec((1, LANES), lambda i, r: (0, 0))],
        out_specs=[pl.BlockSpec((1, ROW_TILE, d), lambda i, r: (i, r, 0)),
                   pl.BlockSpec((1, ROW_TILE, LANES), lambda i, r: (i, r, 0)),
                   pl.BlockSpec((1, ROW_TILE, LANES), lambda i, r: (i, r, 0))],
        out_shape=[jax.ShapeDtypeStruct((b, s, d), BF16),
                   jax.ShapeDtypeStruct((b, s, LANES), jnp.int32),
                   jax.ShapeDtypeStruct((b, s, LANES), F32)],
        compiler_params=_cparams(("parallel", "parallel")),
        name="norm_mod_router",
    )(h, gain.reshape(1, d), mods, rw, rb)


def _gelu_exact(x):
    return 0.5 * x * (1.0 + lax.erf(x * (2.0 ** -0.5)))


def _mm_kernel(*refs, nk, mode):
    n_in = {"swiglu": 3, "resid": 4}.get(mode, 2)
    a_ref, w_ref = refs[:2]
    o_ref = refs[n_in]
    if mode == "swiglu":
        w2_ref = refs[2]
    if mode == "resid":
        h_ref, g_ref = refs[2:4]
    if nk > 1:
        acc_ref = refs[n_in + 1]
        if mode == "swiglu":
            acc2_ref = refs[n_in + 2]
    k = pl.program_id(2)
    a = a_ref[...]
    p = jnp.dot(a, w_ref[...].astype(BF16), preferred_element_type=F32)
    if mode == "swiglu":
        p2 = jnp.dot(a, w2_ref[...].astype(BF16), preferred_element_type=F32)

    if nk > 1:
        @pl.when(k == 0)
        def _():
            acc_ref[...] = p
            if mode == "swiglu":
                acc2_ref[...] = p2

        @pl.when(k > 0)
        def _():
            acc_ref[...] += p
            if mode == "swiglu":
                acc2_ref[...] += p2

    def finish():
        r = acc_ref[...] if nk > 1 else p
        if mode == "plain":
            o_ref[...] = r.astype(o_ref.dtype)
        elif mode == "gelu":
            o_ref[...] = _gelu_exact(r).astype(o_ref.dtype)
        elif mode == "swiglu":
            r2 = acc2_ref[...] if nk > 1 else p2
            o_ref[...] = (_silu(r) * r2).astype(o_ref.dtype)
        else:
            o_ref[...] = h_ref[...] + g_ref[0] * r

    if nk > 1:
        pl.when(k == nk - 1)(finish)
    else:
        finish()


def _mm(a, w, *, mode, tm, tn, tk, out_dtype, n_out=None, w_col0=0, w2_col0=0, h=None, gate=None,
        rows_per_gate=None):
    m, kdim = a.shape
    n_out = w.shape[1] if n_out is None else n_out
    nk = kdim // tk
    assert m % tm == 0 and n_out % tn == 0 and kdim % tk == 0 and w_col0 % tn == 0 and w2_col0 % tn == 0
    cj, cj2 = w_col0 // tn, w2_col0 // tn
    in_specs = [pl.BlockSpec((tm, tk), lambda i, j, k: (i, k)),
                pl.BlockSpec((tk, tn), lambda i, j, k: (k, j + cj))]
    args = [a, w]
    n_acc = 0 if nk == 1 else (2 if mode == "swiglu" else 1)
    scratch = [pltpu.VMEM((tm, tn), F32)] * n_acc
    if mode == "swiglu":
        in_specs.append(pl.BlockSpec((tk, tn), lambda i, j, k: (k, j + cj2)))
        args.append(w)
    if mode == "resid":
        tiles_per_gate = rows_per_gate // tm
        in_specs += [pl.BlockSpec((tm, tn), lambda i, j, k: (i, j)),
                     pl.BlockSpec((1, 1, tn), lambda i, j, k: (i // tiles_per_gate, 0, j))]
        args += [h, gate]
    return pl.pallas_call(
        functools.partial(_mm_kernel, nk=nk, mode=mode),
        grid=(m // tm, n_out // tn, nk),
        in_specs=in_specs,
        out_specs=pl.BlockSpec((tm, tn), lambda i, j, k: (i, j)),
        out_shape=jax.ShapeDtypeStruct((m, n_out), out_dtype),
        scratch_shapes=scratch,
        compiler_params=_cparams(("parallel", "parallel", "arbitrary")),
        name="mm_" + mode,
    )(*args)


def _conv_kernel(x_ref, w_ref, o_ref, xp_ref, *, n_ctx, n_q_blocks, n_k_blocks):
    s = x_ref.shape[1]
    half = GDN_CONV // 2
    w = [w_ref[i:i + 1, :] for i in range(GDN_CONV)]
    cb = pl.program_id(1)
    q_scale = GDN_HEAD_DIM ** -0.5
    pad = CONV_PAD_ROWS
    zeros = jnp.zeros((pad, LANES), F32)
    base = pad
    for lo, hi in ((0, n_ctx), (n_ctx, s)):
        n = hi - lo
        xp_ref[base - pad:base, :] = zeros
        xp_ref[base:base + n, :] = x_ref[0, lo:hi, :].astype(F32)
        xp_ref[base + n:base + n + pad, :] = zeros
        y = xp_ref[base:base + n, :] * w[half]
        for o in range(-half, half + 1):
            if o != 0:
                y = y + xp_ref[base + o:base + o + n, :] * w[o + half]
        base += n + pad
        y = _silu(y)
        fac = lax.rsqrt(jnp.sum(y * y, axis=-1, keepdims=True) + EPS)
        scale = jnp.where(cb < n_q_blocks, fac * q_scale, jnp.where(cb < n_q_blocks + n_k_blocks, fac, 1.0))
        o_ref[0, lo:hi, :] = (y * scale).astype(o_ref.dtype)


def _gdn_conv(proj, conv_w, n_ctx, key_dim):
    b, s, _ = proj.shape
    c = conv_w.shape[1]
    nqb = key_dim // LANES
    return pl.pallas_call(
        functools.partial(_conv_kernel, n_ctx=n_ctx, n_q_blocks=nqb, n_k_blocks=nqb),
        grid=(b, c // LANES),
        in_specs=[pl.BlockSpec((1, s, LANES), lambda i, j: (i, 0, j)),
                  pl.BlockSpec((GDN_CONV, LANES), lambda i, j: (0, j))],
        out_specs=pl.BlockSpec((1, s, LANES), lambda i, j: (i, 0, j)),
        out_shape=jax.ShapeDtypeStruct((b, s, c), BF16),
        scratch_shapes=[pltpu.VMEM((s + 3 * CONV_PAD_ROWS, LANES), F32)],
        compiler_params=_cparams(("parallel", "parallel")),
        name="gdn_conv",
    )(proj, conv_w)


def _gates_kernel(ba_ref, alog_ref, dtb_ref, o_ref, *, n_dir_lanes):
    x = ba_ref[0]
    lc = lax.broadcasted_iota(jnp.int32, (GDN_CHUNK, LANES), 1)
    half = 2 * n_dir_lanes
    beta = 1.0 / (1.0 + jnp.exp(-x))
    z = x + dtb_ref[...]
    softplus = jnp.maximum(z, 0.0) + jnp.log1p(jnp.exp(-jnp.abs(z)))
    g = -jnp.exp(alog_ref[...]) * softplus
    ri = lax.broadcasted_iota(jnp.int32, (GDN_CHUNK, GDN_CHUNK), 0)
    ci = lax.broadcasted_iota(jnp.int32, (GDN_CHUNK, GDN_CHUNK), 1)
    lower = (ri >= ci).astype(F32)
    upper = (ri <= ci).astype(F32)
    for c in range(x.shape[0] // GDN_CHUNK):
        rows = slice(c * GDN_CHUNK, (c + 1) * GDN_CHUNK)
        gc = g[rows]
        fwd = jnp.dot(lower, gc, preferred_element_type=F32, precision=HIGHEST)
        bwd = jnp.dot(upper, gc, preferred_element_type=F32, precision=HIGHEST)
        gcs =jnp.where(lc < half + n_dir_lanes, fwd, bwd)
        o_ref[0, rows, :] = jnp.where(lc < half, beta[rows], gcs)


def _gdn_gates(ba, a_log, dt_bias):
    b, s, _ = ba.shape
    nd = a_log.shape[-1]
    zeros = jnp.zeros((2 * nd,), F32)
    alog = jnp.concatenate([zeros, a_log.reshape(-1)]).reshape(1, LANES)
    dtb = jnp.concatenate([zeros, dt_bias.reshape(-1)]).reshape(1, LANES)
    return pl.pallas_call(
        functools.partial(_gates_kernel, n_dir_lanes=nd),
        grid=(b, s // GDN_STEP),
        in_specs=[pl.BlockSpec((1, GDN_STEP, LANES), lambda i, r: (i, r, 0)),
                  pl.BlockSpec((1, LANES), lambda i, r: (0, 0)),
                  pl.BlockSpec((1, LANES), lambda i, r: (0, 0))],
        out_specs=pl.BlockSpec((1, GDN_STEP, LANES), lambda i, r: (i, r, 0)),
        out_shape=jax.ShapeDtypeStruct((b, s, LANES), F32),
        compiler_params=_cparams(("parallel", "parallel")),
        name="gdn_gates",
    )(ba, alog, dtb)


def _gdn_kernel(q_ref, k_ref, v_ref, col_ref, o_ref, s_ref, *, direction, n_dir_lanes):
    hg = pl.program_id(1)
    step = pl.program_id(2)
    hq = GDN_QK_HEADS_PER_STEP
    nv = 2 * hq
    c = GDN_CHUNK
    dh = GDN_HEAD_DIM
    n_chunks = GDN_STEP // c

    @pl.when(step == 0)
    def _():
        s_ref[...] = jnp.zeros_like(s_ref)

    col = col_ref[0]
    lane0 = direction * n_dir_lanes + hg * nv
    beta_g = pltpu.roll(col, (LANES - lane0) % LANES, 1)
    gcs_g = pltpu.roll(col, (LANES // 2 - lane0) % LANES, 1)
    gcs_r = gcs_g.T

    ri = lax.broadcasted_iota(jnp.int32, (c, c), 0)
    ci = lax.broadcasted_iota(jnp.int32, (c, c), 1)
    if direction == 0:
        incl, strict = ri >= ci, ri > ci
    else:
        incl, strict = ri <= ci, ri < ci
    eye_b = (ri == ci).astype(BF16)
    blockdiag = (ri // 16) == (ci // 16)
    order = range(n_chunks) if direction == 0 else range(n_chunks - 1, -1, -1)
    heads = range(nv)
    order = list(order)
    groups = [order[i:i + GDN_CHUNKS_PER_STAGE] for i in range(0, n_chunks, GDN_CHUNKS_PER_STAGE)]
    for grp in groups:
        rows = {p: slice(p * c, (p + 1) * c) for p in grp}
        pq = [(p, qh) for p in grp for qh in range(hq)]
        pj = [(p, j) for p in grp for j in heads]
        ks = {(p, qh): k_ref[0, rows[p], qh * dh:(qh + 1) * dh] for p, qh in pq}
        qs = {(p, qh): q_ref[0, rows[p], qh * dh:(qh + 1) * dh] for p, qh in pq}
        kq = {x: _dot_nt(jnp.concatenate([ks[x], qs[x]], axis=0), ks[x]) for x in pq}
        kk = {x: kq[x][:c] for x in pq}
        qk = {x: kq[x][c:] for x in pq}
        k_t = {x: ks[x].astype(F32).T for x in pq}
        bcol = {(p, j): beta_g[rows[p], j:j + 1] for p, j in pj}
        gcol = {(p, j): gcs_g[rows[p], j:j + 1] for p, j in pj}
        grow = {(p, j): gcs_r[j:j + 1, rows[p]] for p, j in pj}
        glast = {x: (g[:, c - 1:c] if direction == 0 else g[:, 0:1]) for x, g in grow.items()}
        decay = {x: jnp.exp(jnp.where(incl, gcol[x] - grow[x], -jnp.inf)) for x in pj}
        a = {(p, j): jnp.where(strict, kk[p, j // 2] * bcol[p, j] * decay[p, j], 0.0) for p, j in pj}
        a_df = {x: jnp.where(blockdiag, a[x], 0.0) for x in pj}
        a_o = {x: (a[x] - a_df[x]).astype(BF16) for x in pj}
        a_d = {x: a_df[x].astype(BF16) for x in pj}
        p2 = {x: _dotb(a_d[x], a_d[x]) for x in pj}
        p4 = {x: _dotb(p2[x], p2[x]) for x in pj}
        p8 = {x: _dotb(p4[x], p4[x]) for x in pj}
        t1 = {x: _dotb(eye_b - a_d[x], eye_b + p2[x]) for x in pj}
        t2 = {x: _dotb(t1[x], eye_b + p4[x]) for x in pj}
        t_d = {x: _dotb(t2[x], eye_b + p8[x]) for x in pj}
        nn = {x: _dotb(t_d[x], a_o[x]) for x in pj}
        n2 = {x: _dotb(nn[x], nn[x]) for x in pj}
        rr = {x: _dotb(eye_b - nn[x], eye_b + n2[x]) for x in pj}
        t = {x: _dotb(rr[x], t_d[x]) for x in pj}
        rhs = {(p, j): jnp.concatenate([v_ref[0, rows[p], j * dh:(j + 1) * dh].astype(F32) * bcol[p, j],
                                        ks[p, j // 2].astype(F32) * (bcol[p, j] * jnp.exp(gcol[p, j]))],
                                       axis=1).astype(BF16)
               for p, j in pj}
        uw = {x: _dot(t[x], rhs[x]) for x in pj}
        wq_lhs = {(p, j): jnp.concatenate([uw[p, j][:, dh:], qs[p, j // 2].astype(F32) * jnp.exp(gcol[p, j])],
                                          axis=0).astype(BF16)
                  for p, j in pj}
        lhs2 = {(p, j): jnp.concatenate([jnp.where(incl, qk[p, j // 2] * decay[p, j], 0.0),
                                         k_t[p, j // 2] * jnp.exp(glast[p, j] - grow[p, j])], axis=0).astype(BF16)
                for p, j in pj}
        for p in grp:
            state = [s_ref[j] for j in heads]
            wq = [_dot(wq_lhs[p, j], state[j]) for j in heads]
            v_new = [uw[p, j][:, :dh] - wq[j][:c] for j in heads]
            r2 = [_dot(lhs2[p, j], v_new[j]) for j in heads]
            for j in heads:
                o_ref[0, rows[p], j * dh:(j + 1) * dh] = (wq[j][c:] + r2[j][:c]).astype(o_ref.dtype)
                s_ref[j] = state[j] * jnp.exp(glast[p, j]) + r2[j][c:]


def _gdn_scan(qkv, col, direction, n_ctx, key_dim, n_v_heads):
    b, s, _ = qkv.shape
    hq = GDN_QK_HEADS_PER_STEP
    nv = 2 * hq
    n_steps = s // GDN_STEP
    n_ctx_steps = n_ctx // GDN_STEP
    assert n_ctx_steps == 1
    qw = hq * GDN_HEAD_DIM
    vw = nv * GDN_HEAD_DIM
    n_groups = key_dim // qw
    k_blk0 = key_dim // qw
    v_blk0 = 2 * key_dim // vw

    def blk(st):
        if direction == 0:
            return st
        return jnp.where(st == 0, 0, n_steps - st)

    def oblk(st):
        return blk(jnp.maximum(st, 1)) - 1

    return pl.pallas_call(
        functools.partial(_gdn_kernel, direction=direction, n_dir_lanes=n_v_heads),
        grid=(b, n_groups, n_steps),
        in_specs=[pl.BlockSpec((1, GDN_STEP, qw), lambda i, g, st: (i, blk(st), g)),
                  pl.BlockSpec((1, GDN_STEP, qw), lambda i, g, st: (i, blk(st), k_blk0 + g)),
                  pl.BlockSpec((1, GDN_STEP, vw), lambda i, g, st: (i, blk(st), v_blk0 + g)),
                  pl.BlockSpec((1, GDN_STEP, LANES), lambda i, g, st: (i, blk(st), 0))],
        out_specs=pl.BlockSpec((1, GDN_STEP, vw), lambda i, g, st: (i, oblk(st), g)),
        out_shape=jax.ShapeDtypeStruct((b, s - n_ctx, n_v_heads * GDN_HEAD_DIM), BF16),
        scratch_shapes=[pltpu.VMEM((nv, GDN_HEAD_DIM, GDN_HEAD_DIM), F32)],
        compiler_params=_cparams(("parallel", "parallel", "arbitrary")),
        name="gdn_scan_d%d" % direction,
    )(qkv, qkv, qkv, col)


def _gdn_out_kernel(o0_ref, o1_ref, z_ref, g_ref, y_ref):
    dh = GDN_HEAD_DIM
    for hd in range(o0_ref.shape[2] // dh):
        sl = slice(hd * dh, (hd + 1) * dh)
        o = o0_ref[0, :, sl].astype(F32) + o1_ref[0, :, sl].astype(F32)
        y = o * lax.rsqrt(jnp.mean(o * o, axis=-1, keepdims=True) + EPS) * g_ref[...]
        y_ref[0, :, sl] = (y * _silu(z_ref[0, :, sl].astype(F32))).astype(y_ref.dtype)


def _gdn_out_norm(o0, o1, proj, o_norm, n_ctx, z_col0):
    b, l, w = o0.shape
    ctx_tiles = n_ctx // ROW_TILE
    zb = z_col0 // w
    return pl.pallas_call(
        _gdn_out_kernel,
        grid=(b, l // ROW_TILE),
        in_specs=[pl.BlockSpec((1, ROW_TILE, w), lambda i, r: (i, r, 0)),
                  pl.BlockSpec((1, ROW_TILE, w), lambda i, r: (i, r, 0)),
                  pl.BlockSpec((1, ROW_TILE, w), lambda i, r: (i, r + ctx_tiles, zb)),
                  pl.BlockSpec((1, GDN_HEAD_DIM), lambda i, r: (0, 0))],
        out_specs=pl.BlockSpec((1, ROW_TILE, w), lambda i, r: (i, r, 0)),
        out_shape=jax.ShapeDtypeStruct((b, l, w), BF16),
        compiler_params=_cparams(("parallel", "parallel")),
        name="gdn_out_norm",
    )(o0, o1, proj, o_norm.reshape(1, -1))


def _cm_kernel(gu_ref, gv_ref, lg_ref, lb_ref, ws_ref, bs_ref, o_ref):
    gd = gu_ref.shape[1] // CM_GROUPS
    for ch in range(gu_ref.shape[0] // CM_CHUNK):
        rows = slice(ch * CM_CHUNK, (ch + 1) * CM_CHUNK)
        gv = gv_ref[rows, :].astype(F32)
        mu = jnp.mean(gv, axis=-1, keepdims=True)
        xc = gv - mu
        var = jnp.mean(xc * xc, axis=-1, keepdims=True)
        gvn = xc * lax.rsqrt(var + EPS) * lg_ref[...] + lb_ref[...]
        for g in range(CM_GROUPS):
            sl = slice(g * gd, (g + 1) * gd)
            mixed = _dot(ws_ref[g], gvn[:, sl]) + bs_ref[:, g:g + 1]
            o_ref[rows, sl] = (gu_ref[rows, sl].astype(F32) * mixed).astype(o_ref.dtype)


def _cm_spatial(guv, ln_g, ln_b, w_s, b_s):
    t, w2 = guv.shape
    w = w2 // 2
    rt = 2 * CM_CHUNK
    return pl.pallas_call(
        _cm_kernel,
        grid=(t // rt,),
        in_specs=[pl.BlockSpec((rt, w), lambda i: (i, 0)),
                  pl.BlockSpec((rt, w), lambda i: (i, 1)),
                  pl.BlockSpec((1, w), lambda i: (0, 0)),
                  pl.BlockSpec((1, w), lambda i: (0, 0)),
                  pl.BlockSpec((CM_GROUPS, CM_CHUNK, CM_CHUNK), lambda i: (0, 0, 0)),
                  pl.BlockSpec((CM_CHUNK, CM_GROUPS), lambda i: (0, 0))],
        out_specs=pl.BlockSpec((rt, w), lambda i: (i, 0)),
        out_shape=jax.ShapeDtypeStruct((t, w), BF16),
        compiler_params=_cparams(("parallel",)),
        name="cm_spatial",
    )(guv, guv, ln_g.reshape(1, w), ln_b.reshape(1, w), w_s, jnp.transpose(b_s))


def _moe_plan(idx2):
    t = idx2.shape[0]
    a = t * TOP_K
    e = idx2.reshape(a)
    onehot = (e[:, None] == jnp.arange(N_EXPERTS, dtype=jnp.int32)[None, :]).astype(jnp.int32)
    rank = jnp.sum((jnp.cumsum(onehot, axis=0) - onehot) * onehot, axis=1)
    counts = jnp.sum(onehot, axis=0)
    padded = ((counts + MOE_TM - 1) // MOE_TM) * MOE_TM
    ends = jnp.cumsum(padded)
    starts = ends - padded
    pos = starts[e] + rank
    a_pad = a + N_EXPERTS * MOE_TM
    row_token = jnp.zeros((a_pad,), jnp.int32).at[pos].set(jnp.arange(a, dtype=jnp.int32) // TOP_K,
                                                           unique_indices=True)
    n_tiles = a_pad // MOE_TM
    tile_start = jnp.arange(n_tiles, dtype=jnp.int32) * MOE_TM
    tile_expert = jnp.minimum(jnp.searchsorted(ends, tile_start, side="right"), N_EXPERTS - 1).astype(jnp.int32)
    tile_rows = jnp.clip((starts + counts)[tile_expert] - tile_start, 0, MOE_TM).astype(jnp.int32)
    tile_expert = jnp.where(tile_rows > 0, tile_expert, tile_expert[ends[-1] // MOE_TM - 1])
    return pos.reshape(t, TOP_K), row_token, tile_expert, tile_rows


def _gather_kernel(idx_ref, src_ref, dst_ref, sem):
    def issue(r, carry):
        pltpu.make_async_copy(src_ref.at[idx_ref[0, 0, r]], dst_ref.at[r], sem).start()
        return carry

    lax.fori_loop(0, GATHER_ROWS, issue, 0, unroll=DMA_UNROLL)

    def drain(r, carry):
        pltpu.make_async_copy(src_ref.at[0], dst_ref.at[0], sem).wait()
        return carry

    lax.fori_loop(0, GATHER_ROWS, drain, 0, unroll=DMA_UNROLL)


def _gather_rows(src3, row_idx):
    r = row_idx.shape[0]
    nb = r // GATHER_ROWS
    return pl.pallas_call(
        _gather_kernel,
        grid=(nb,),
        in_specs=[pl.BlockSpec((1, 1, GATHER_ROWS), lambda i: (i, 0, 0), memory_space=pltpu.SMEM),
                  pl.BlockSpec(memory_space=pl.ANY)],
        out_specs=pl.BlockSpec((GATHER_ROWS,) + src3.shape[1:], lambda i: (i, 0, 0)),
        out_shape=jax.ShapeDtypeStruct((r,) + src3.shape[1:], src3.dtype),
        scratch_shapes=[pltpu.SemaphoreType.DMA(())],
        compiler_params=_cparams(("arbitrary",)),
        name="gather_rows",
    )(row_idx.reshape(nb, 1, GATHER_ROWS), src3)


def _moe_gu_kernel(te_ref, tr_ref, x2_ref, wg_ref, wu_ref, o_ref, wb_ref):
    rows = tr_ref[pl.program_id(0)]
    tm = o_ref.shape[0]

    def swiglu(x, wg, wu):
        g = jnp.dot(x, wg, preferred_element_type=F32)
        u = jnp.dot(x, wu, preferred_element_type=F32)
        return (_silu(g) * u).astype(o_ref.dtype)

    @pl.when(rows == tm)
    def _():
        o_ref[...] = swiglu(x2_ref[...], wg_ref[0].astype(BF16), wu_ref[0].astype(BF16))

    @pl.when((rows > 0) & (rows < tm))
    def _():
        wb_ref[0] = wg_ref[0].astype(BF16)
        wb_ref[1] = wu_ref[0].astype(BF16)
        for sb in range(tm // MOE_SUB):
            sl = slice(sb * MOE_SUB, (sb + 1) * MOE_SUB)

            @pl.when(sb * MOE_SUB < rows)
            def _():
                o_ref[sl, :] = swiglu(x2_ref[sl, :], wb_ref[0], wb_ref[1])

            @pl.when(sb * MOE_SUB >= rows)
            def _():
                o_ref[sl, :] = jnp.zeros((MOE_SUB, o_ref.shape[1]), o_ref.dtype)

    @pl.when(rows == 0)
    def _():
        o_ref[...] = jnp.zeros_like(o_ref)


def _moe_gu(xs, w_gu, tile_expert, tile_rows, tn=512):
    a_pad, d = xs.shape
    f = w_gu.shape[2] // 2
    nj = f // tn

    def wj(i, j, tr):
        return jnp.where(tr[i] > 0, j, nj - 1)

    return pl.pallas_call(
        _moe_gu_kernel,
        grid_spec=pltpu.PrefetchScalarGridSpec(
            num_scalar_prefetch=2,
            grid=(a_pad // MOE_TM, nj),
            in_specs=[pl.BlockSpec((MOE_TM, d), lambda i, j, te, tr: (i, 0)),
                      pl.BlockSpec((1, d, tn), lambda i, j, te, tr: (te[i], 0, wj(i, j, tr))),
                      pl.BlockSpec((1, d, tn), lambda i, j, te, tr: (te[i], 0, wj(i, j, tr) + nj))],
            out_specs=pl.BlockSpec((MOE_TM, tn), lambda i, j, te, tr: (i, j)),
            scratch_shapes=[pltpu.VMEM((2, d, tn), BF16)]),
        out_shape=jax.ShapeDtypeStruct((a_pad, f), BF16),
        compiler_params=_cparams(("parallel", "parallel")),
        name="moe_gu",
    )(tile_expert, tile_rows, xs, w_gu, w_gu)


def _moe_down_kernel(te_ref, tr_ref, a_ref, w_ref, o_ref, acc_ref, wb_ref, *, nk):
    k = pl.program_id(2)
    rows = tr_ref[pl.program_id(0)]
    tm = acc_ref.shape[0]

    @pl.when(rows == tm)
    def _():
        p = jnp.dot(a_ref[...], w_ref[0].astype(BF16), preferred_element_type=F32)

        @pl.when(k == 0)
        def _():
            acc_ref[...] = p

        @pl.when(k > 0)
        def _():
            acc_ref[...] += p

    @pl.when((rows > 0) & (rows < tm))
    def _():
        @pl.when(k == 0)
        def _():
            acc_ref[...] = jnp.zeros_like(acc_ref)

        wb_ref[...] = w_ref[0].astype(BF16)
        for sb in range(tm // MOE_SUB):
            sl = slice(sb * MOE_SUB, (sb + 1) * MOE_SUB)

            @pl.when(sb * MOE_SUB < rows)
            def _():
                acc_ref[sl, :] += jnp.dot(a_ref[sl, :], wb_ref[...], preferred_element_type=F32)

    @pl.when((rows > 0) & (k == nk - 1))
    def _():
        o_ref[...] = acc_ref[...].astype(o_ref.dtype)

    @pl.when((rows == 0) & (k == nk - 1))
    def _():
        o_ref[...] = jnp.zeros_like(o_ref)


def _moe_down(hmid, w_down, tile_expert, tile_rows, tn=1024, tk=1792):
    a_pad, f = hmid.shape
    d = w_down.shape[2]
    nk = f // tk
    nj = d // tn

    def live(i, tr, idx, last):
        return jnp.where(tr[i] > 0, idx, last)

    return pl.pallas_call(
        functools.partial(_moe_down_kernel, nk=nk),
        grid_spec=pltpu.PrefetchScalarGridSpec(
            num_scalar_prefetch=2,
            grid=(a_pad // MOE_TM, nj, nk),
            in_specs=[pl.BlockSpec((MOE_TM, tk), lambda i, j, k, te, tr: (i, live(i, tr, k, nk - 1))),
                      pl.BlockSpec((1, tk, tn),
                                   lambda i, j, k, te, tr: (te[i], live(i, tr, k, nk - 1), live(i, tr, j, nj - 1)))],
            out_specs=pl.BlockSpec((MOE_TM, tn), lambda i, j, k, te, tr: (i, j)),
            scratch_shapes=[pltpu.VMEM((MOE_TM, tn), F32), pltpu.VMEM((tk, tn), BF16)]),
        out_shape=jax.ShapeDtypeStruct((a_pad, d), BF16),
        compiler_params=_cparams(("parallel", "parallel", "arbitrary")),
        name="moe_down",
    )(tile_expert, tile_rows, hmid, w_down)


def _combine_kernel(pos_ref, posn_ref, y_ref, wt_ref, h_ref, gt_ref, gn_ref, o_ref, buf_ref, sem):
    i = pl.program_id(0)
    slot = i % 2
    n = buf_ref.shape[1]

    def gather(idx_ref, sl):
        def issue(r, carry):
            pltpu.make_async_copy(y_ref.at[idx_ref[0, 0, r]], buf_ref.at[sl, r], sem.at[sl]).start()
            return carry

        lax.fori_loop(0, n, issue, 0, unroll=DMA_UNROLL)

    @pl.when(i == 0)
    def _():
        gather(pos_ref, slot)

    @pl.when(i + 1 < pl.num_programs(0))
    def _():
        gather(posn_ref, 1 - slot)

    def drain(r, carry):
        pltpu.make_async_copy(y_ref.at[0], buf_ref.at[slot, 0], sem.at[slot]).wait()
        return carry

    lax.fori_loop(0, n, drain, 0, unroll=DMA_UNROLL)
    tt = o_ref.shape[0]
    w0 = wt_ref[:, 0:1]
    w1 = wt_ref[:, 1:2]
    ss = jnp.zeros((tt, 1), F32)
    for s in range(buf_ref.shape[2]):
        sl = slice(s * LANES, (s + 1) * LANES)
        y = (w0 * buf_ref[slot, pl.ds(0, tt), s, :].astype(F32)
             + w1 * buf_ref[slot, pl.ds(tt, tt), s, :].astype(F32))
        hn = h_ref[:, sl] + gt_ref[0, :, sl] * y
        ss = ss + jnp.sum(hn * hn, axis=-1, keepdims=True)
        o_ref[:, sl] = hn
    o_ref[...] = o_ref[...] * lax.rsqrt(ss * (1.0 / o_ref.shape[1]) + EPS) * gn_ref[...]


def _moe_combine_final(y3, pos_tiles, wt, h, gate, gain, tokens_per_batch, tt=256):
    t, d = h.shape
    sd = d // LANES
    tiles_per_batch = tokens_per_batch // tt
    n_steps = t // tt
    return pl.pallas_call(
        _combine_kernel,
        grid=(n_steps,),
        in_specs=[pl.BlockSpec((1, 1, TOP_K * tt), lambda i: (i, 0, 0), memory_space=pltpu.SMEM),
                  pl.BlockSpec((1, 1, TOP_K * tt), lambda i: (jnp.minimum(i + 1, n_steps - 1), 0, 0),
                               memory_space=pltpu.SMEM),
                  pl.BlockSpec(memory_space=pl.ANY),
                  pl.BlockSpec((tt, LANES), lambda i: (i, 0)),
                  pl.BlockSpec((tt, d), lambda i: (i, 0)),
                  pl.BlockSpec((1, 1, d), lambda i: (i // tiles_per_batch, 0, 0)),
                  pl.BlockSpec((1, d), lambda i: (0, 0))],
        out_specs=pl.BlockSpec((tt, d), lambda i: (i, 0)),
        out_shape=jax.ShapeDtypeStruct((t, d), F32),
        scratch_shapes=[pltpu.VMEM((2, TOP_K * tt, sd, LANES), y3.dtype), pltpu.SemaphoreType.DMA((2,))],
        compiler_params=_cparams(("arbitrary",)),
        name="moe_combine",
    )(pos_tiles, pos_tiles, y3, wt, h, gate, gain.reshape(1, d))


def kernel(x, c, ctx, c_ctx, ada_w, ada_b, norm_mix, norm_ffn, norm_final, gdn_w_in, gdn_conv, gdn_a_log,
           gdn_dt_bias, gdn_o_norm, gdn_w_out, cm_w_in, cm_ln_g, cm_ln_b, cm_w_s, cm_b_s, cm_w_out, ffn_w_gu,
           ffn_w_down, moe_router, moe_router_b, moe_w_gu, moe_w_down):
    b, l, d = x.shape
    n_ctx = ctx.shape[1]
    s = n_ctx + l
    t = b * l
    n_v_heads = gdn_a_log.shape[-1]
    val_dim = n_v_heads * GDN_HEAD_DIM
    key_dim = (gdn_conv.shape[-1] - val_dim) // 2
    conv_dim = 2 * key_dim + val_dim
    ffn_dim = ffn_w_down.shape[1]

    cond8 = jnp.zeros((8, d), F32).at[:b].set(c).at[b].set(c_ctx)
    mods = _ada(cond8, ada_w, ada_b).reshape(ada_w.shape[0], 8, 6, d)

    def seg_mods(layer, first):
        lat = mods[layer, :b, first:first + 2]
        cx = jnp.broadcast_to(mods[layer, b, first:first + 2][None], lat.shape)
        return jnp.stack([cx, lat], axis=1)

    u = _norm_mod(x, norm_mix[0], seg_mods(0, 0), ctx).reshape(b * s, d)
    w_in = gdn_w_in[0]
    n_qkvz = conv_dim + val_dim
    tm_cat = _tile(b * s, MM_TM)
    tm = _tile(l, MM_TM)
    proj = _mm(u, w_in, mode="plain", tm=tm_cat, tn=1024, tk=d, out_dtype=BF16, n_out=n_qkvz).reshape(b, s, n_qkvz)
    ba = _mm(u, w_in, mode="plain", tm=tm_cat, tn=LANES, tk=d, out_dtype=F32, n_out=LANES, w_col0=n_qkvz)
    qkv = _gdn_conv(proj, gdn_conv[0], n_ctx, key_dim)
    col = _gdn_gates(ba.reshape(b, s, LANES), gdn_a_log[0], gdn_dt_bias[0])
    o_fwd = _gdn_scan(qkv, col, 0, n_ctx, key_dim, n_v_heads)
    o_bwd = _gdn_scan(qkv, col, 1, n_ctx, key_dim, n_v_heads)
    y = _gdn_out_norm(o_fwd, o_bwd, proj, gdn_o_norm[0], n_ctx, conv_dim).reshape(t, val_dim)
    gate = lambda layer, idx: mods[layer, :b, idx].reshape(b, 1, d)
    h = _mm(y, gdn_w_out[0], mode="resid", tm=tm,tn=512, tk=val_dim, out_dtype=F32, h=x.reshape(t, d),
            gate=gate(0, 2), rows_per_gate=l)

    tf = _norm_mod(h.reshape(b, l, d), norm_ffn[0], seg_mods(0, 3)).reshape(t, d)
    mid = _mm(tf, ffn_w_gu[0], mode="swiglu", tm=tm,tn=512, tk=d, out_dtype=BF16, n_out=ffn_dim, w2_col0=ffn_dim)
    h = _mm(mid, ffn_w_down[0], mode="resid", tm=tm,tn=256, tk=ffn_dim, out_dtype=F32, h=h, gate=gate(0, 5),
            rows_per_gate=l)

    u = _norm_mod(h.reshape(b, l, d), norm_mix[1], seg_mods(1, 0)).reshape(t, d)
    guv = _mm(u, cm_w_in[0], mode="gelu", tm=tm, tn=1024, tk=d, out_dtype=BF16)
    cmix = _cm_spatial(guv, cm_ln_g[0], cm_ln_b[0], cm_w_s[0], cm_b_s[0])
    h = _mm(cmix, cm_w_out[0], mode="resid", tm=tm,tn=512, tk=cmix.shape[1], out_dtype=F32, h=h, gate=gate(1, 2),
            rows_per_gate=l)

    tf, idx, wt = _norm_mod_router(h.reshape(b, l, d), norm_ffn[1], seg_mods(1, 3), moe_router[0], moe_router_b[0])
    pos, row_token, tile_expert, tile_rows = _moe_plan(idx.reshape(t, LANES)[:, :TOP_K])
    sd = d // LANES
    xs = _gather_rows(tf.reshape(t, sd, LANES), row_token).reshape(-1, d)
    hmid = _moe_gu(xs, moe_w_gu[0], tile_expert, tile_rows)
    ys3 = _moe_down(hmid, moe_w_down[0], tile_expert, tile_rows).reshape(-1, sd, LANES)
    tt = 256
    pos_tiles = jnp.transpose(pos.reshape(t // tt, tt, TOP_K), (0, 2, 1)).reshape(t // tt, 1, TOP_K * tt)
    out = _moe_combine_final(ys3, pos_tiles, wt.reshape(t, LANES), h, gate(1, 5), norm_final, l, tt)
    return out.reshape(b, l, d)
```

```python
import functools

import jax
import jax.numpy as jnp
from jax import lax
from jax.experimental import pallas as pl
from jax.experimental.pallas import tpu as pltpu

F32 = jnp.float32
BF16 = jnp.bfloat16
EPS = 1e-6
HIGHEST = lax.Precision.HIGHEST

LANES = 128
VMEM_LIMIT_BYTES = 56 * 1024 * 1024

GDN_HEAD_DIM = 128
GDN_CHUNK = 64
GDN_CONV = 5
CONV_PAD_ROWS = 8
CM_CHUNK = 128
CM_GROUPS = 16
N_EXPERTS = 8
TOP_K = 2

ROW_TILE = 256
GDN_STEP = 256
GDN_QK_HEADS_PER_STEP = 8
GDN_CHUNKS_PER_STAGE = 4
MM_TM = 1024
MOE_TM = 1024
MOE_SUB = 256
DMA_UNROLL = 8


def _cparams(sem):
    return pltpu.CompilerParams(dimension_semantics=sem, vmem_limit_bytes=VMEM_LIMIT_BYTES)


def _tile(n, preferred):
    t = preferred
    while n % t:
        t //= 2
    return t


def _silu(x):
    return x * (1.0 / (1.0 + jnp.exp(-x)))


def _dot(a, b):
    return jnp.dot(a.astype(BF16), b.astype(BF16), preferred_element_type=F32)


def _dotb(a, b):
    return jnp.dot(a, b, preferred_element_type=F32).astype(BF16)


def _dot_nt(a, b):
    return lax.dot_general(a.astype(BF16), b.astype(BF16), (((1,), (1,)), ((), ())),
                           preferred_element_type=F32)


def _ada_kernel(c_ref, w_ref, b_ref, o_ref):
    s = _silu(c_ref[...])
    o_ref[0] = jnp.dot(s, w_ref[0], preferred_element_type=F32, precision=HIGHEST) + b_ref[0]


def _ada(cond8, ada_w, ada_b, tn=1024):
    depth, d, n = ada_w.shape
    return pl.pallas_call(
        _ada_kernel,
        grid=(depth, n // tn),
        in_specs=[pl.BlockSpec((8, d), lambda l, j: (0, 0)),
                  pl.BlockSpec((1, d, tn), lambda l, j: (l, 0, j)),
                  pl.BlockSpec((1, 1, tn), lambda l, j: (l, 0, j))],
        out_specs=pl.BlockSpec((1, 8, tn), lambda l, j: (l, 0, j)),
        out_shape=jax.ShapeDtypeStruct((depth, 8, n), F32),
        compiler_params=_cparams(("parallel", "parallel")),
        name="ada",
    )(cond8, ada_w, ada_b.reshape(depth, 1, n))


def _norm_mod_kernel(*refs, n_ctx_tiles):
    if n_ctx_tiles:
        c_ref, h_ref, g_ref, m_ref, o_ref = refs
        x = jnp.where(pl.program_id(1) < n_ctx_tiles, c_ref[0], h_ref[0])
    else:
        h_ref, g_ref, m_ref, o_ref = refs
        x = h_ref[0]
    y = x * lax.rsqrt(jnp.mean(x * x, axis=-1, keepdims=True) + EPS) * g_ref[...]
    o_ref[0] = (y * (1.0 + m_ref[0, 0, 1:2, :]) + m_ref[0, 0, 0:1, :]).astype(o_ref.dtype)


def _norm_mod(h, gain, mods, ctx=None):
    b, l, d = h.shape
    nct = 0 if ctx is None else ctx.shape[1] // ROW_TILE
    in_specs = [pl.BlockSpec((1, ROW_TILE, d), lambda i, r: (i, jnp.maximum(r - nct, 0), 0)),
                pl.BlockSpec((1, d), lambda i, r: (0, 0)),
                pl.BlockSpec((1, 1, 2, d), lambda i, r: (i, jnp.where(r < nct, 0, 1), 0, 0))]
    args = [h, gain.reshape(1, d), mods]
    if nct:
        in_specs.insert(0, pl.BlockSpec((1, ROW_TILE, d), lambda i, r: (i, jnp.minimum(r, nct - 1), 0)))
        args.insert(0, ctx)
    return pl.pallas_call(
        functools.partial(_norm_mod_kernel, n_ctx_tiles=nct),
        grid=(b, nct + l // ROW_TILE),
        in_specs=in_specs,
        out_specs=pl.BlockSpec((1, ROW_TILE, d), lambda i, r: (i, r, 0)),
        out_shape=jax.ShapeDtypeStruct((b, nct * ROW_TILE + l, d), BF16),
        compiler_params=_cparams(("parallel", "parallel")),
        name="norm_mod",
    )(*args)


def _norm_mod_router_kernel(h_ref, g_ref, m_ref, rw_ref, rb_ref, o_ref, idx_ref, wt_ref):
    x = h_ref[0]
    y = x * lax.rsqrt(jnp.mean(x * x, axis=-1, keepdims=True) + EPS) * g_ref[...]
    t = y * (1.0 + m_ref[0, 0, 1:2, :]) + m_ref[0, 0, 0:1, :]
    o_ref[0] = t.astype(o_ref.dtype)
    logits = jnp.dot(t, rw_ref[...], preferred_element_type=F32, precision=HIGHEST) + rb_ref[...]
    lane = lax.broadcasted_iota(jnp.int32, logits.shape, 1)
    m1 = jnp.max(logits, axis=-1, keepdims=True)
    i1 = jnp.min(jnp.where(logits == m1, lane, LANES), axis=-1, keepdims=True)
    rest = jnp.where(lane == i1, -jnp.inf, logits)
    m2 = jnp.max(rest, axis=-1, keepdims=True)
    i2 = jnp.min(jnp.where(rest == m2, lane, LANES), axis=-1, keepdims=True)
    e = jnp.exp(m2 - m1)
    w1 = 1.0 / (1.0 + e)
    w2 = e * w1
    idx_ref[0] = jnp.where(lane == 0, i1, jnp.where(lane == 1, i2, 0))
    wt_ref[0] = jnp.where(lane == 0, w1, jnp.where(lane == 1, w2, 0.0))


def _norm_mod_router(h, gain, mods, router_w, router_b):
    b, s, d = h.shape
    ne = router_w.shape[1]
    rw = jnp.zeros((d, LANES), F32).at[:, :ne].set(router_w)
    rb = jnp.full((1, LANES), -1e30, F32).at[0, :ne].set(router_b)
    return pl.pallas_call(
        _norm_mod_router_kernel,
        grid=(b, s // ROW_TILE),
        in_specs=[pl.BlockSpec((1, ROW_TILE, d), lambda i, r: (i, r, 0)),
                  pl.BlockSpec((1, d), lambda i, r: (0, 0)),
                  pl.BlockSpec((1, 1, 2, d), lambda i, r: (i, 1, 0, 0)),
                  pl.BlockSpec((d, LANES), lambda i, r: (0, 0)),
                  pl.BlockSpec((1, LANES), lambda i, r: (0, 0))],
        out_specs=[pl.BlockSpec((1, ROW_TILE, d), lambda i, r: (i, r, 0)),
                   pl.BlockSpec((1, ROW_TILE, LANES), lambda i, r: (i, r, 0)),
                   pl.BlockSpec((1, ROW_TILE, LANES), lambda i, r: (i, r, 0))],
        out_shape=[jax.ShapeDtypeStruct((b, s, d), BF16),
                   jax.ShapeDtypeStruct((b, s, LANES), jnp.int32),
                   jax.ShapeDtypeStruct((b, s, LANES), F32)],
        compiler_params=_cparams(("parallel", "parallel")),
        name="norm_mod_router",
    )(h, gain.reshape(1, d), mods, rw, rb)


def _gelu_exact(x):
    return 0.5 * x * (1.0 + lax.erf(x * (2.0 ** -0.5)))


def _mm_kernel(*refs, nk, mode):
    n_in = {"swiglu": 3, "resid": 4}.get(mode, 2)
    a_ref, w_ref = refs[:2]
    o_ref = refs[n_in]
    if mode == "swiglu":
        w2_ref = refs[2]
    if mode == "resid":
        h_ref, g_ref = refs[2:4]
    if nk > 1:
        acc_ref = refs[n_in + 1]
        if mode == "swiglu":
            acc2_ref = refs[n_in + 2]
    k = pl.program_id(2)
    a = a_ref[...]
    p = jnp.dot(a, w_ref[...].astype(BF16), preferred_element_type=F32)
    if mode == "swiglu":
        p2 = jnp.dot(a, w2_ref[...].astype(BF16), preferred_element_type=F32)

    if nk > 1:
        @pl.when(k == 0)
        def _():
            acc_ref[...] = p
            if mode == "swiglu":
                acc2_ref[...] = p2

        @pl.when(k > 0)
        def _():
            acc_ref[...] += p
            if mode == "swiglu":
                acc2_ref[...] += p2

    def finish():
        r = acc_ref[...] if nk > 1 else p
        if mode == "plain":
            o_ref[...] = r.astype(o_ref.dtype)
        elif mode == "gelu":
            o_ref[...] = _gelu_exact(r).astype(o_ref.dtype)
        elif mode == "swiglu":
            r2 = acc2_ref[...] if nk > 1 else p2
            o_ref[...] = (_silu(r) * r2).astype(o_ref.dtype)
        else:
            o_ref[...] = h_ref[...] + g_ref[0] * r

    if nk > 1:
        pl.when(k == nk - 1)(finish)
    else:
        finish()


def _mm(a, w, *, mode, tm, tn, tk, out_dtype, n_out=None, w_col0=0, w2_col0=0, h=None, gate=None,
        rows_per_gate=None):
    m, kdim = a.shape
    n_out = w.shape[1] if n_out is None else n_out
    nk = kdim // tk
    assert m % tm == 0 and n_out % tn == 0 and kdim % tk == 0 and w_col0 % tn == 0 and w2_col0 % tn == 0
    cj, cj2 = w_col0 // tn, w2_col0 // tn
    in_specs = [pl.BlockSpec((tm, tk), lambda i, j, k: (i, k)),
                pl.BlockSpec((tk, tn), lambda i, j, k: (k, j + cj))]
    args = [a, w]
    n_acc = 0 if nk == 1 else (2 if mode == "swiglu" else 1)
    scratch = [pltpu.VMEM((tm, tn), F32)] * n_acc
    if mode == "swiglu":
        in_specs.append(pl.BlockSpec((tk, tn), lambda i, j, k: (k, j + cj2)))
        args.append(w)
    if mode == "resid":
        tiles_per_gate = rows_per_gate // tm
        in_specs += [pl.BlockSpec((tm, tn), lambda i, j, k: (i, j)),
                     pl.BlockSpec((1, 1, tn), lambda i, j, k: (i // tiles_per_gate, 0, j))]
        args += [h, gate]
    return pl.pallas_call(
        functools.partial(_mm_kernel, nk=nk, mode=mode),
        grid=(m // tm, n_out // tn, nk),
        in_specs=in_specs,
        out_specs=pl.BlockSpec((tm, tn), lambda i, j, k: (i, j)),
        out_shape=jax.ShapeDtypeStruct((m, n_out), out_dtype),
        scratch_shapes=scratch,
        compiler_params=_cparams(("parallel", "parallel", "arbitrary")),
        name="mm_" + mode,
    )(*args)


def _conv_kernel(x_ref, w_ref, o_ref, xp_ref, *, n_ctx, n_q_blocks, n_k_blocks):
    s = x_ref.shape[1]
    half = GDN_CONV // 2
    w = [w_ref[i:i + 1, :] for i in range(GDN_CONV)]
    cb = pl.program_id(1)
    q_scale = GDN_HEAD_DIM ** -0.5
    pad = CONV_PAD_ROWS
    zeros = jnp.zeros((pad, LANES), F32)
    base = pad
    for lo, hi in ((0, n_ctx), (n_ctx, s)):
        n = hi - lo
        xp_ref[base - pad:base, :] = zeros
        xp_ref[base:base + n, :] = x_ref[0, lo:hi, :].astype(F32)
        xp_ref[base + n:base + n + pad, :] = zeros
        y = xp_ref[base:base + n, :] * w[half]
        for o in range(-half, half + 1):
            if o != 0:
                y = y + xp_ref[base + o:base + o + n, :] * w[o + half]
        base += n + pad
        y = _silu(y)
        fac = lax.rsqrt(jnp.sum(y * y, axis=-1, keepdims=True) + EPS)
        scale = jnp.where(cb < n_q_blocks, fac * q_scale, jnp.where(cb < n_q_blocks + n_k_blocks, fac, 1.0))
        o_ref[0, lo:hi, :] = (y * scale).astype(o_ref.dtype)


def _gdn_conv(proj, conv_w, n_ctx, key_dim):
    b, s, _ = proj.shape
    c = conv_w.shape[1]
    nqb = key_dim // LANES
    return pl.pallas_call(
        functools.partial(_conv_kernel, n_ctx=n_ctx, n_q_blocks=nqb, n_k_blocks=nqb),
        grid=(b, c // LANES),
        in_specs=[pl.BlockSpec((1, s, LANES), lambda i, j: (i, 0, j)),
                  pl.BlockSpec((GDN_CONV, LANES), lambda i, j: (0, j))],
        out_specs=pl.BlockSpec((1, s, LANES), lambda i, j: (i, 0, j)),
        out_shape=jax.ShapeDtypeStruct((b, s, c), BF16),
        scratch_shapes=[pltpu.VMEM((s + 3 * CONV_PAD_ROWS, LANES), F32)],
        compiler_params=_cparams(("parallel", "parallel")),
        name="gdn_conv",
    )(proj, conv_w)


def _gates_kernel(ba_ref, alog_ref, dtb_ref, o_ref, *, n_dir_lanes):
    x = ba_ref[0]
    lc = lax.broadcasted_iota(jnp.int32, (GDN_CHUNK, LANES), 1)
    half = 2 * n_dir_lanes
    beta = 1.0 / (1.0 + jnp.exp(-x))
    z = x + dtb_ref[...]
    softplus = jnp.maximum(z, 0.0) + jnp.log1p(jnp.exp(-jnp.abs(z)))
    g = -jnp.exp(alog_ref[...]) * softplus
    ri = lax.broadcasted_iota(jnp.int32, (GDN_CHUNK, GDN_CHUNK), 0)
    ci = lax.broadcasted_iota(jnp.int32, (GDN_CHUNK, GDN_CHUNK), 1)
    lower = (ri >= ci).astype(F32)
    upper = (ri <= ci).astype(F32)
    for c in range(x.shape[0] // GDN_CHUNK):
        rows = slice(c * GDN_CHUNK, (c + 1) * GDN_CHUNK)
        gc = g[rows]
        fwd = jnp.dot(lower, gc, preferred_element_type=F32, precision=HIGHEST)
        bwd = jnp.dot(upper, gc, preferred_element_type=F32, precision=HIGHEST)
        gcs =jnp.where(lc < half + n_dir_lanes, fwd, bwd)
        o_ref[0, rows, :] = jnp.where(lc < half, beta[rows], gcs)


def _gdn_gates(ba, a_log, dt_bias):
    b, s, _ = ba.shape
    nd = a_log.shape[-1]
    zeros = jnp.zeros((2 * nd,), F32)
    alog = jnp.concatenate([zeros, a_log.reshape(-1)]).reshape(1, LANES)
    dtb = jnp.concatenate([zeros, dt_bias.reshape(-1)]).reshape(1, LANES)
    return pl.pallas_call(
        functools.partial(_gates_kernel, n_dir_lanes=nd),
        grid=(b, s // GDN_STEP),
        in_specs=[pl.BlockSpec((1, GDN_STEP, LANES), lambda i, r: (i, r, 0)),
                  pl.BlockSpec((1, LANES), lambda i, r: (0, 0)),
                  pl.BlockSpec((1, LANES), lambda i, r: (0, 0))],
        out_specs=pl.BlockSpec((1, GDN_STEP, LANES), lambda i, r: (i, r, 0)),
        out_shape=jax.ShapeDtypeStruct((b, s, LANES), F32),
        compiler_params=_cparams(("parallel", "parallel")),
        name="gdn_gates",
    )(ba, alog, dtb)


def _gdn_kernel(q_ref, k_ref, v_ref, col_ref, o_ref, s_ref, *, direction, n_dir_lanes):
    hg = pl.program_id(1)
    step = pl.program_id(2)
    hq = GDN_QK_HEADS_PER_STEP
    nv = 2 * hq
    c = GDN_CHUNK
    dh = GDN_HEAD_DIM
    n_chunks = GDN_STEP // c

    @pl.when(step == 0)
    def _():
        s_ref[...] = jnp.zeros_like(s_ref)

    col = col_ref[0]
    lane0 = direction * n_dir_lanes + hg * nv
    beta_g = pltpu.roll(col, (LANES - lane0) % LANES, 1)
    gcs_g = pltpu.roll(col, (LANES // 2 - lane0) % LANES, 1)
    gcs_r = gcs_g.T

    ri = lax.broadcasted_iota(jnp.int32, (c, c), 0)
    ci = lax.broadcasted_iota(jnp.int32, (c, c), 1)
    if direction == 0:
        incl, strict = ri >= ci, ri > ci
    else:
        incl, strict = ri <= ci, ri < ci
    eye_b = (ri == ci).astype(BF16)
    blockdiag = (ri // 16) == (ci // 16)
    order = range(n_chunks) if direction == 0 else range(n_chunks - 1, -1, -1)
    heads = range(nv)
    order = list(order)
    groups = [order[i:i + GDN_CHUNKS_PER_STAGE] for i in range(0, n_chunks, GDN_CHUNKS_PER_STAGE)]
    for grp in groups:
        rows = {p: slice(p * c, (p + 1) * c) for p in grp}
        pq = [(p, qh) for p in grp for qh in range(hq)]
        pj = [(p, j) for p in grp for j in heads]
        ks = {(p, qh): k_ref[0, rows[p], qh * dh:(qh + 1) * dh] for p, qh in pq}
        qs = {(p, qh): q_ref[0, rows[p], qh * dh:(qh + 1) * dh] for p, qh in pq}
        kq = {x: _dot_nt(jnp.concatenate([ks[x], qs[x]], axis=0), ks[x]) for x in pq}
        kk = {x: kq[x][:c] for x in pq}
        qk = {x: kq[x][c:] for x in pq}
        k_t = {x: ks[x].astype(F32).T for x in pq}
        bcol = {(p, j): beta_g[rows[p], j:j + 1] for p, j in pj}
        gcol = {(p, j): gcs_g[rows[p], j:j + 1] for p, j in pj}
        grow = {(p, j): gcs_r[j:j + 1, rows[p]] for p, j in pj}
        glast = {x: (g[:, c - 1:c] if direction == 0 else g[:, 0:1]) for x, g in grow.items()}
        decay = {x: jnp.exp(jnp.where(incl, gcol[x] - grow[x], -jnp.inf)) for x in pj}
        a = {(p, j): jnp.where(strict, kk[p, j // 2] * bcol[p, j] * decay[p, j], 0.0) for p, j in pj}
        a_df = {x: jnp.where(blockdiag, a[x], 0.0) for x in pj}
        a_o = {x: (a[x] - a_df[x]).astype(BF16) for x in pj}
        a_d = {x: a_df[x].astype(BF16) for x in pj}
        p2 = {x: _dotb(a_d[x], a_d[x]) for x in pj}
        p4 = {x: _dotb(p2[x], p2[x]) for x in pj}
        p8 = {x: _dotb(p4[x], p4[x]) for x in pj}
        t1 = {x: _dotb(eye_b - a_d[x], eye_b + p2[x]) for x in pj}
        t2 = {x: _dotb(t1[x], eye_b + p4[x]) for x in pj}
        t_d = {x: _dotb(t2[x], eye_b + p8[x]) for x in pj}
        nn = {x: _dotb(t_d[x], a_o[x]) for x in pj}
        n2 = {x: _dotb(nn[x], nn[x]) for x in pj}
        rr = {x: _dotb(eye_b - nn[x], eye_b + n2[x]) for x in pj}
        t = {x: _dotb(rr[x], t_d[x]) for x in pj}
        rhs = {(p, j): jnp.concatenate([v_ref[0, rows[p], j * dh:(j + 1) * dh].astype(F32) * bcol[p, j],
                                        ks[p, j // 2].astype(F32) * (bcol[p, j] * jnp.exp(gcol[p, j]))],
                                       axis=1).astype(BF16)
               for p, j in pj}
        uw = {x: _dot(t[x], rhs[x]) for x in pj}
        wq_lhs = {(p, j): jnp.concatenate([uw[p, j][:, dh:], qs[p, j // 2].astype(F32) * jnp.exp(gcol[p, j])],
                                          axis=0).astype(BF16)
                  for p, j in pj}
        lhs2 = {(p, j): jnp.concatenate([jnp.where(incl, qk[p, j // 2] * decay[p, j], 0.0),
                                         k_t[p, j // 2] * jnp.exp(glast[p, j] - grow[p, j])], axis=0).astype(BF16)
                for p, j in pj}
        for p in grp:
            state = [s_ref[j] for j in heads]
            wq = [_dot(wq_lhs[p, j], state[j]) for j in heads]
            v_new = [uw[p, j][:, :dh] - wq[j][:c] for j in heads]
            r2 = [_dot(lhs2[p, j], v_new[j]) for j in heads]
            for j in heads:
                o_ref[0, rows[p], j * dh:(j + 1) * dh] = (wq[j][c:] + r2[j][:c]).astype(o_ref.dtype)
                s_ref[j] = state[j] * jnp.exp(glast[p, j]) + r2[j][c:]


def _gdn_scan(qkv, col, direction, n_ctx, key_dim, n_v_heads):
    b, s, _ = qkv.shape
    hq = GDN_QK_HEADS_PER_STEP
    nv = 2 * hq
    n_steps = s // GDN_STEP
    n_ctx_steps = n_ctx // GDN_STEP
    assert n_ctx_steps == 1
    qw = hq * GDN_HEAD_DIM
    vw = nv * GDN_HEAD_DIM
    n_groups = key_dim // qw
    k_blk0 = key_dim // qw
    v_blk0 = 2 * key_dim // vw

    def blk(st):
        if direction == 0:
            return st
        return jnp.where(st == 0, 0, n_steps - st)

    def oblk(st):
        return blk(jnp.maximum(st, 1)) - 1

    return pl.pallas_call(
        functools.partial(_gdn_kernel, direction=direction, n_dir_lanes=n_v_heads),
        grid=(b, n_groups, n_steps),
        in_specs=[pl.BlockSpec((1, GDN_STEP, qw), lambda i, g, st: (i, blk(st), g)),
                  pl.BlockSpec((1, GDN_STEP, qw), lambda i, g, st: (i, blk(st), k_blk0 + g)),
                  pl.BlockSpec((1, GDN_STEP, vw), lambda i, g, st: (i, blk(st), v_blk0 + g)),
                  pl.BlockSpec((1, GDN_STEP, LANES), lambda i, g, st: (i, blk(st), 0))],
        out_specs=pl.BlockSpec((1, GDN_STEP, vw), lambda i, g, st: (i, oblk(st), g)),
        out_shape=jax.ShapeDtypeStruct((b, s - n_ctx, n_v_heads * GDN_HEAD_DIM), BF16),
        scratch_shapes=[pltpu.VMEM((nv, GDN_HEAD_DIM, GDN_HEAD_DIM), F32)],
        compiler_params=_cparams(("parallel", "parallel", "arbitrary")),
        name="gdn_scan_d%d" % direction,
    )(qkv, qkv, qkv, col)


def _gdn_out_kernel(o0_ref, o1_ref, z_ref, g_ref, y_ref):
    dh = GDN_HEAD_DIM
    for hd in range(o0_ref.shape[2] // dh):
        sl = slice(hd * dh, (hd + 1) * dh)
        o = o0_ref[0, :, sl].astype(F32) + o1_ref[0, :, sl].astype(F32)
        y = o * lax.rsqrt(jnp.mean(o * o, axis=-1, keepdims=True) + EPS) * g_ref[...]
        y_ref[0, :, sl] = (y * _silu(z_ref[0, :, sl].astype(F32))).astype(y_ref.dtype)


def _gdn_out_norm(o0, o1, proj, o_norm, n_ctx, z_col0):
    b, l, w = o0.shape
    ctx_tiles = n_ctx // ROW_TILE
    zb = z_col0 // w
    return pl.pallas_call(
        _gdn_out_kernel,
        grid=(b, l // ROW_TILE),
        in_specs=[pl.BlockSpec((1, ROW_TILE, w), lambda i, r: (i, r, 0)),
                  pl.BlockSpec((1, ROW_TILE, w), lambda i, r: (i, r, 0)),
                  pl.BlockSpec((1, ROW_TILE, w), lambda i, r: (i, r + ctx_tiles, zb)),
                  pl.BlockSpec((1, GDN_HEAD_DIM), lambda i, r: (0, 0))],
        out_specs=pl.BlockSpec((1, ROW_TILE, w), lambda i, r: (i, r, 0)),
        out_shape=jax.ShapeDtypeStruct((b, l, w), BF16),
        compiler_params=_cparams(("parallel", "parallel")),
        name="gdn_out_norm",
    )(o0, o1, proj, o_norm.reshape(1, -1))


def _cm_kernel(gu_ref, gv_ref, lg_ref, lb_ref, ws_ref, bs_ref, o_ref):
    gd = gu_ref.shape[1] // CM_GROUPS
    for ch in range(gu_ref.shape[0] // CM_CHUNK):
        rows = slice(ch * CM_CHUNK, (ch + 1) * CM_CHUNK)
        gv = gv_ref[rows, :].astype(F32)
        mu = jnp.mean(gv, axis=-1, keepdims=True)
        xc = gv - mu
        var = jnp.mean(xc * xc, axis=-1, keepdims=True)
        gvn = xc * lax.rsqrt(var + EPS) * lg_ref[...] + lb_ref[...]
        for g in range(CM_GROUPS):
            sl = slice(g * gd, (g + 1) * gd)
            mixed = _dot(ws_ref[g], gvn[:, sl]) + bs_ref[:, g:g + 1]
            o_ref[rows, sl] = (gu_ref[rows, sl].astype(F32) * mixed).astype(o_ref.dtype)


def _cm_spatial(guv, ln_g, ln_b, w_s, b_s):
    t, w2 = guv.shape
    w = w2 // 2
    rt = 2 * CM_CHUNK
    return pl.pallas_call(
        _cm_kernel,
        grid=(t // rt,),
        in_specs=[pl.BlockSpec((rt, w), lambda i: (i, 0)),
                  pl.BlockSpec((rt, w), lambda i: (i, 1)),
                  pl.BlockSpec((1, w), lambda i: (0, 0)),
                  pl.BlockSpec((1, w), lambda i: (0, 0)),
                  pl.BlockSpec((CM_GROUPS, CM_CHUNK, CM_CHUNK), lambda i: (0, 0, 0)),
                  pl.BlockSpec((CM_CHUNK, CM_GROUPS), lambda i: (0, 0))],
        out_specs=pl.BlockSpec((rt, w), lambda i: (i, 0)),
        out_shape=jax.ShapeDtypeStruct((t, w), BF16),
        compiler_params=_cparams(("parallel",)),
        name="cm_spatial",
    )(guv, guv, ln_g.reshape(1, w), ln_b.reshape(1, w), w_s, jnp.transpose(b_s))


def _moe_plan(idx2):
    t = idx2.shape[0]
    a = t * TOP_K
    e = idx2.reshape(a)
    onehot = (e[:, None] == jnp.arange(N_EXPERTS, dtype=jnp.int32)[None, :]).astype(jnp.int32)
    rank = jnp.sum((jnp.cumsum(onehot, axis=0) - onehot) * onehot, axis=1)
    counts = jnp.sum(onehot, axis=0)
    padded = ((counts + MOE_TM - 1) // MOE_TM) * MOE_TM
    ends = jnp.cumsum(padded)
    starts = ends - padded
    pos = starts[e] + rank
    a_pad = a + N_EXPERTS * MOE_TM
    n_tiles = a_pad // MOE_TM
    tile_start = jnp.arange(n_tiles, dtype=jnp.int32) * MOE_TM
    tile_expert = jnp.minimum(jnp.searchsorted(ends, tile_start, side="right"), N_EXPERTS - 1).astype(jnp.int32)
    tile_rows = jnp.clip((starts + counts)[tile_expert] - tile_start, 0, MOE_TM).astype(jnp.int32)
    tile_expert = jnp.where(tile_rows > 0, tile_expert, tile_expert[ends[-1] // MOE_TM - 1])
    return pos.reshape(t, TOP_K), a_pad, tile_expert, tile_rows


def _scatter_kernel(pos_ref, src_ref, init_ref, dst_ref, sem):
    del init_ref
    tt = src_ref.shape[0]

    def issue(r, carry):
        for k in range(TOP_K):
            pltpu.make_async_copy(src_ref.at[r], dst_ref.at[pos_ref[0, 0, k * tt + r]], sem).start()
        return carry

    lax.fori_loop(0, tt, issue, 0, unroll=DMA_UNROLL)

    def drain(r, carry):
        pltpu.make_async_copy(src_ref.at[0], dst_ref.at[0], sem).wait()
        return carry

    lax.fori_loop(0, TOP_K * tt, drain, 0, unroll=DMA_UNROLL)


def _scatter_rows(src3, pos_tiles, n_rows):
    nb, _, n_pos = pos_tiles.shape
    tt = n_pos // TOP_K
    out_shape = jax.ShapeDtypeStruct((n_rows,) + src3.shape[1:], src3.dtype)
    return pl.pallas_call(
        _scatter_kernel,
        grid=(nb,),
        in_specs=[pl.BlockSpec((1, 1, n_pos), lambda i: (i, 0, 0), memory_space=pltpu.SMEM),
                  pl.BlockSpec((tt,) + src3.shape[1:], lambda i: (i, 0, 0)),
                  pl.BlockSpec(memory_space=pl.ANY)],
        out_specs=pl.BlockSpec(memory_space=pl.ANY),
        out_shape=out_shape,
        input_output_aliases={2: 0},
        scratch_shapes=[pltpu.SemaphoreType.DMA(())],
        compiler_params=_cparams(("arbitrary",)),
        name="scatter_rows",
    )(pos_tiles, src3, jnp.zeros(out_shape.shape, out_shape.dtype))


def _moe_gu_kernel(te_ref, tr_ref, x2_ref, wg_ref, wu_ref, o_ref, wb_ref):
    rows = tr_ref[pl.program_id(0)]
    tm = o_ref.shape[0]

    def swiglu(x, wg, wu):
        g = jnp.dot(x, wg, preferred_element_type=F32)
        u = jnp.dot(x, wu, preferred_element_type=F32)
        return (_silu(g) * u).astype(o_ref.dtype)

    @pl.when(rows == tm)
    def _():
        o_ref[...] = swiglu(x2_ref[...], wg_ref[0].astype(BF16), wu_ref[0].astype(BF16))

    @pl.when((rows > 0) & (rows < tm))
    def _():
        wb_ref[0] = wg_ref[0].astype(BF16)
        wb_ref[1] = wu_ref[0].astype(BF16)
        for sb in range(tm // MOE_SUB):
            sl = slice(sb * MOE_SUB, (sb + 1) * MOE_SUB)

            @pl.when(sb * MOE_SUB < rows)
            def _():
                o_ref[sl, :] = swiglu(x2_ref[sl, :], wb_ref[0], wb_ref[1])

            @pl.when(sb * MOE_SUB >= rows)
            def _():
                o_ref[sl, :] = jnp.zeros((MOE_SUB, o_ref.shape[1]), o_ref.dtype)

    @pl.when(rows == 0)
    def _():
        o_ref[...] = jnp.zeros_like(o_ref)


def _moe_gu(xs, w_gu, tile_expert, tile_rows, tn=512):
    a_pad, d = xs.shape
    f = w_gu.shape[2] // 2
    nj = f // tn

    def wj(i, j, tr):
        return jnp.where(tr[i] > 0, j, nj - 1)

    return pl.pallas_call(
        _moe_gu_kernel,
        grid_spec=pltpu.PrefetchScalarGridSpec(
            num_scalar_prefetch=2,
            grid=(a_pad // MOE_TM, nj),
            in_specs=[pl.BlockSpec((MOE_TM, d), lambda i, j, te, tr: (i, 0)),
                      pl.BlockSpec((1, d, tn), lambda i, j, te, tr: (te[i], 0, wj(i, j, tr))),
                      pl.BlockSpec((1, d, tn), lambda i, j, te, tr: (te[i], 0, wj(i, j, tr) + nj))],
            out_specs=pl.BlockSpec((MOE_TM, tn), lambda i, j, te, tr: (i, j)),
            scratch_shapes=[pltpu.VMEM((2, d, tn), BF16)]),
        out_shape=jax.ShapeDtypeStruct((a_pad, f), BF16),
        compiler_params=_cparams(("parallel", "parallel")),
        name="moe_gu",
    )(tile_expert, tile_rows, xs, w_gu, w_gu)


def _moe_down_kernel(te_ref, tr_ref, a_ref, w_ref, o_ref, acc_ref, wb_ref, *, nk):
    k = pl.program_id(2)
    rows = tr_ref[pl.program_id(0)]
    tm = acc_ref.shape[0]

    @pl.when(rows == tm)
    def _():
        p = jnp.dot(a_ref[...], w_ref[0].astype(BF16), preferred_element_type=F32)

        @pl.when(k == 0)
        def _():
            acc_ref[...] = p

        @pl.when(k > 0)
        def _():
            acc_ref[...] += p

    @pl.when((rows > 0) & (rows < tm))
    def _():
        @pl.when(k == 0)
        def _():
            acc_ref[...] = jnp.zeros_like(acc_ref)

        wb_ref[...] = w_ref[0].astype(BF16)
        for sb in range(tm // MOE_SUB):
            sl = slice(sb * MOE_SUB, (sb + 1) * MOE_SUB)

            @pl.when(sb * MOE_SUB < rows)
            def _():
                acc_ref[sl, :] += jnp.dot(a_ref[sl, :], wb_ref[...], preferred_element_type=F32)

    @pl.when((rows > 0) & (k == nk - 1))
    def _():
        o_ref[...] = acc_ref[...].astype(o_ref.dtype)

    @pl.when((rows == 0) & (k == nk - 1))
    def _():
        o_ref[...] = jnp.zeros_like(o_ref)


def _moe_down(hmid, w_down, tile_expert, tile_rows, tn=1024, tk=1792):
    a_pad, f = hmid.shape
    d = w_down.shape[2]
    nk = f // tk
    nj = d // tn

    def live(i, tr, idx, last):
        return jnp.where(tr[i] > 0, idx, last)

    return pl.pallas_call(
        functools.partial(_moe_down_kernel, nk=nk),
        grid_spec=pltpu.PrefetchScalarGridSpec(
            num_scalar_prefetch=2,
            grid=(a_pad // MOE_TM, nj, nk),
            in_specs=[pl.BlockSpec((MOE_TM, tk), lambda i, j, k, te, tr: (i, live(i, tr, k, nk - 1))),
                      pl.BlockSpec((1, tk, tn),
                                   lambda i, j, k, te, tr: (te[i], live(i, tr, k, nk - 1), live(i, tr, j, nj - 1)))],
            out_specs=pl.BlockSpec((MOE_TM, tn), lambda i, j, k, te, tr: (i, j)),
            scratch_shapes=[pltpu.VMEM((MOE_TM, tn), F32), pltpu.VMEM((tk, tn), BF16)]),
        out_shape=jax.ShapeDtypeStruct((a_pad, d), BF16),
        compiler_params=_cparams(("parallel", "parallel", "arbitrary")),
        name="moe_down",
    )(tile_expert, tile_rows, hmid, w_down)


def _combine_kernel(pos_ref, posn_ref, y_ref, wt_ref, h_ref, gt_ref, gn_ref, o_ref, buf_ref, sem):
    i = pl.program_id(0)
    slot = i % 2
    n = buf_ref.shape[1]

    def gather(idx_ref, sl):
        def issue(r, carry):
            pltpu.make_async_copy(y_ref.at[idx_ref[0, 0, r]], buf_ref.at[sl, r], sem.at[sl]).start()
            return carry

        lax.fori_loop(0, n, issue, 0, unroll=DMA_UNROLL)

    @pl.when(i == 0)
    def _():
        gather(pos_ref, slot)

    @pl.when(i + 1 < pl.num_programs(0))
    def _():
        gather(posn_ref, 1 - slot)

    def drain(r, carry):
        pltpu.make_async_copy(y_ref.at[0], buf_ref.at[slot, 0], sem.at[slot]).wait()
        return carry

    lax.fori_loop(0, n, drain, 0, unroll=DMA_UNROLL)
    tt = o_ref.shape[0]
    w0 = wt_ref[:, 0:1]
    w1 = wt_ref[:, 1:2]
    ss = jnp.zeros((tt, 1), F32)
    for s in range(buf_ref.shape[2]):
        sl = slice(s * LANES, (s + 1) * LANES)
        y = (w0 * buf_ref[slot, pl.ds(0, tt), s, :].astype(F32)
             + w1 * buf_ref[slot, pl.ds(tt, tt), s, :].astype(F32))
        hn = h_ref[:, sl] + gt_ref[0, :, sl] * y
        ss = ss + jnp.sum(hn * hn, axis=-1, keepdims=True)
        o_ref[:, sl] = hn
    o_ref[...] = o_ref[...] * lax.rsqrt(ss * (1.0 / o_ref.shape[1]) + EPS) * gn_ref[...]


def _moe_combine_final(y3, pos_tiles, wt, h, gate, gain, tokens_per_batch, tt=256):
    t, d = h.shape
    sd = d // LANES
    tiles_per_batch = tokens_per_batch // tt
    n_steps = t // tt
    return pl.pallas_call(
        _combine_kernel,
        grid=(n_steps,),
        in_specs=[pl.BlockSpec((1, 1, TOP_K * tt), lambda i: (i, 0, 0), memory_space=pltpu.SMEM),
                  pl.BlockSpec((1, 1, TOP_K * tt), lambda i: (jnp.minimum(i + 1, n_steps - 1), 0, 0),
                               memory_space=pltpu.SMEM),
                  pl.BlockSpec(memory_space=pl.ANY),
                  pl.BlockSpec((tt, LANES), lambda i: (i, 0)),
                  pl.BlockSpec((tt, d), lambda i: (i, 0)),
                  pl.BlockSpec((1, 1, d), lambda i: (i // tiles_per_batch, 0, 0)),
                  pl.BlockSpec((1, d), lambda i: (0, 0))],
        out_specs=pl.BlockSpec((tt, d), lambda i: (i, 0)),
        out_shape=jax.ShapeDtypeStruct((t, d), F32),
        scratch_shapes=[pltpu.VMEM((2, TOP_K * tt, sd, LANES), y3.dtype), pltpu.SemaphoreType.DMA((2,))],
        compiler_params=_cparams(("arbitrary",)),
        name="moe_combine",
    )(pos_tiles, pos_tiles, y3, wt, h, gate, gain.reshape(1, d))


def kernel(x, c, ctx, c_ctx, ada_w, ada_b, norm_mix, norm_ffn, norm_final, gdn_w_in, gdn_conv, gdn_a_log,
           gdn_dt_bias, gdn_o_norm, gdn_w_out, cm_w_in, cm_ln_g, cm_ln_b, cm_w_s, cm_b_s, cm_w_out, ffn_w_gu,
           ffn_w_down, moe_router, moe_router_b, moe_w_gu, moe_w_down):
    b, l, d = x.shape
    n_ctx = ctx.shape[1]
    s = n_ctx + l
    t = b * l
    n_v_heads = gdn_a_log.shape[-1]
    val_dim = n_v_heads * GDN_HEAD_DIM
    key_dim = (gdn_conv.shape[-1] - val_dim) // 2
    conv_dim = 2 * key_dim + val_dim
    ffn_dim = ffn_w_down.shape[1]

    cond8 = jnp.zeros((8, d), F32).at[:b].set(c).at[b].set(c_ctx)
    mods = _ada(cond8, ada_w, ada_b).reshape(ada_w.shape[0], 8, 6, d)

    def seg_mods(layer, first):
        lat = mods[layer, :b, first:first + 2]
        cx = jnp.broadcast_to(mods[layer, b, first:first + 2][None], lat.shape)
        return jnp.stack([cx, lat], axis=1)

    u = _norm_mod(x, norm_mix[0], seg_mods(0, 0), ctx).reshape(b * s, d)
    w_in = gdn_w_in[0]
    n_qkvz = conv_dim + val_dim
    tm_cat = _tile(b * s, MM_TM)
    tm = _tile(l, MM_TM)
    proj = _mm(u, w_in, mode="plain", tm=tm_cat, tn=1024, tk=d, out_dtype=BF16, n_out=n_qkvz).reshape(b, s, n_qkvz)
    ba = _mm(u, w_in, mode="plain", tm=tm_cat, tn=LANES, tk=d, out_dtype=F32, n_out=LANES, w_col0=n_qkvz)
    qkv = _gdn_conv(proj, gdn_conv[0], n_ctx, key_dim)
    col = _gdn_gates(ba.reshape(b, s, LANES), gdn_a_log[0], gdn_dt_bias[0])
    o_fwd = _gdn_scan(qkv, col, 0, n_ctx, key_dim, n_v_heads)
    o_bwd = _gdn_scan(qkv, col, 1, n_ctx, key_dim, n_v_heads)
    y = _gdn_out_norm(o_fwd, o_bwd, proj, gdn_o_norm[0], n_ctx, conv_dim).reshape(t, val_dim)
    gate = lambda layer, idx: mods[layer, :b, idx].reshape(b, 1, d)
    h = _mm(y, gdn_w_out[0], mode="resid", tm=tm,tn=512, tk=val_dim, out_dtype=F32, h=x.reshape(t, d),
            gate=gate(0, 2), rows_per_gate=l)

    tf = _norm_mod(h.reshape(b, l, d), norm_ffn[0], seg_mods(0, 3)).reshape(t, d)
    mid = _mm(tf, ffn_w_gu[0], mode="swiglu", tm=tm,tn=512, tk=d, out_dtype=BF16, n_out=ffn_dim, w2_col0=ffn_dim)
    h = _mm(mid, ffn_w_down[0], mode="resid", tm=tm,tn=256, tk=ffn_dim, out_dtype=F32, h=h, gate=gate(0, 5),
            rows_per_gate=l)

    u = _norm_mod(h.reshape(b, l, d), norm_mix[1], seg_mods(1, 0)).reshape(t, d)
    guv = _mm(u, cm_w_in[0], mode="gelu", tm=tm, tn=1024, tk=d, out_dtype=BF16)
    cmix = _cm_spatial(guv, cm_ln_g[0], cm_ln_b[0], cm_w_s[0], cm_b_s[0])
    h = _mm(cmix, cm_w_out[0], mode="resid", tm=tm,tn=512, tk=cmix.shape[1], out_dtype=F32, h=h, gate=gate(1, 2),
            rows_per_gate=l)

    tf, idx, wt = _norm_mod_router(h.reshape(b, l, d), norm_ffn[1], seg_mods(1, 3), moe_router[0], moe_router_b[0])
    pos, a_pad, tile_expert, tile_rows = _moe_plan(idx.reshape(t, LANES)[:, :TOP_K])
    tt = ROW_TILE
    pos_tiles = jnp.transpose(pos.reshape(t // tt, tt, TOP_K), (0, 2, 1)).reshape(t // tt, 1, TOP_K * tt)
    sd = d // LANES
    xs = _scatter_rows(tf.reshape(t, sd, LANES), pos_tiles, a_pad).reshape(a_pad, d)
    hmid = _moe_gu(xs, moe_w_gu[0], tile_expert, tile_rows)
    ys3 = _moe_down(hmid, moe_w_down[0], tile_expert, tile_rows).reshape(a_pad, sd, LANES)
    out = _moe_combine_final(ys3, pos_tiles, wt.reshape(t, LANES), h, gate(1, 5), norm_final, l, tt)
    return out.reshape(b, l, d)
```

```python
import functools

import jax
import jax.numpy as jnp
from jax import lax
from jax.experimental import pallas as pl
from jax.experimental.pallas import tpu as pltpu

F32 = jnp.float32
BF16 = jnp.bfloat16
EPS = 1e-6
HIGHEST = lax.Precision.HIGHEST

LANES = 128
VMEM_LIMIT_BYTES = 56 * 1024 * 1024

GDN_HEAD_DIM = 128
GDN_CHUNK = 64
GDN_CONV = 5
CONV_PAD_ROWS = 8
CM_CHUNK = 128
CM_GROUPS = 16
N_EXPERTS = 8
TOP_K = 2

ROW_TILE = 256
GDN_STEP = 256
GDN_QK_HEADS_PER_STEP = 8
GDN_CHUNKS_PER_STAGE = 4
MM_TM = 1024
MOE_TM = 1024
MOE_SUB = 256
DMA_UNROLL = 8


def _cparams(sem):
    return pltpu.CompilerParams(dimension_semantics=sem, vmem_limit_bytes=VMEM_LIMIT_BYTES)


def _tile(n, preferred):
    t = preferred
    while n % t:
        t //= 2
    return t


def _silu(x):
    return x * (1.0 / (1.0 + jnp.exp(-x)))


def _dot(a, b):
    return jnp.dot(a.astype(BF16), b.astype(BF16), preferred_element_type=F32)


def _dotb(a, b):
    return jnp.dot(a, b, preferred_element_type=F32).astype(BF16)


def _dot_nt(a, b):
    return lax.dot_general(a.astype(BF16), b.astype(BF16), (((1,), (1,)), ((), ())),
                           preferred_element_type=F32)


def _ada_kernel(c_ref, w_ref, b_ref, o_ref):
    s = _silu(c_ref[...])
    o_ref[0] = jnp.dot(s, w_ref[0], preferred_element_type=F32, precision=HIGHEST) + b_ref[0]


def _ada(cond8, ada_w, ada_b, tn=1024):
    depth, d, n = ada_w.shape
    return pl.pallas_call(
        _ada_kernel,
        grid=(depth, n // tn),
        in_specs=[pl.BlockSpec((8, d), lambda l, j: (0, 0)),
                  pl.BlockSpec((1, d, tn), lambda l, j: (l, 0, j)),
                  pl.BlockSpec((1, 1, tn), lambda l, j: (l, 0, j))],
        out_specs=pl.BlockSpec((1, 8, tn), lambda l, j: (l, 0, j)),
        out_shape=jax.ShapeDtypeStruct((depth, 8, n), F32),
        compiler_params=_cparams(("parallel", "parallel")),
        name="ada",
    )(cond8, ada_w, ada_b.reshape(depth, 1, n))


def _norm_mod_kernel(*refs, n_ctx_tiles):
    if n_ctx_tiles:
        c_ref, h_ref, g_ref, m_ref, o_ref = refs
        x = jnp.where(pl.program_id(1) < n_ctx_tiles, c_ref[0], h_ref[0])
    else:
        h_ref, g_ref, m_ref, o_ref = refs
        x = h_ref[0]
    y = x * lax.rsqrt(jnp.mean(x * x, axis=-1, keepdims=True) + EPS) * g_ref[...]
    o_ref[0] = (y * (1.0 + m_ref[0, 0, 1:2, :]) + m_ref[0, 0, 0:1, :]).astype(o_ref.dtype)


def _norm_mod(h, gain, mods, ctx=None):
    b, l, d = h.shape
    nct = 0 if ctx is None else ctx.shape[1] // ROW_TILE
    in_specs = [pl.BlockSpec((1, ROW_TILE, d), lambda i, r: (i, jnp.maximum(r - nct, 0), 0)),
                pl.BlockSpec((1, d), lambda i, r: (0, 0)),
                pl.BlockSpec((1, 1, 2, d), lambda i, r: (i, jnp.where(r < nct, 0, 1), 0, 0))]
    args = [h, gain.reshape(1, d), mods]
    if nct:
        in_specs.insert(0, pl.BlockSpec((1, ROW_TILE, d), lambda i, r: (i, jnp.minimum(r, nct - 1), 0)))
        args.insert(0, ctx)
    return pl.pallas_call(
        functools.partial(_norm_mod_kernel, n_ctx_tiles=nct),
        grid=(b, nct + l // ROW_TILE),
        in_specs=in_specs,
        out_specs=pl.BlockSpec((1, ROW_TILE, d), lambda i, r: (i, r, 0)),
        out_shape=jax.ShapeDtypeStruct((b, nct * ROW_TILE + l, d), BF16),
        compiler_params=_cparams(("parallel", "parallel")),
        name="norm_mod",
    )(*args)


def _norm_mod_router_kernel(h_ref, g_ref, m_ref, rw_ref, rb_ref, o_ref, idx_ref, wt_ref):
    x = h_ref[0]
    y = x * lax.rsqrt(jnp.mean(x * x, axis=-1, keepdims=True) + EPS) * g_ref[...]
    t = y * (1.0 + m_ref[0, 0, 1:2, :]) + m_ref[0, 0, 0:1, :]
    o_ref[0] = t.astype(o_ref.dtype)
    logits = jnp.dot(t, rw_ref[...], preferred_element_type=F32, precision=HIGHEST) + rb_ref[...]
    lane = lax.broadcasted_iota(jnp.int32, logits.shape, 1)
    m1 = jnp.max(logits, axis=-1, keepdims=True)
    i1 = jnp.min(jnp.where(logits == m1, lane, LANES), axis=-1, keepdims=True)
    rest = jnp.where(lane == i1, -jnp.inf, logits)
    m2 = jnp.max(rest, axis=-1, keepdims=True)
    i2 = jnp.min(jnp.where(rest == m2, lane, LANES), axis=-1, keepdims=True)
    e = jnp.exp(m2 - m1)
    w1 = 1.0 / (1.0 + e)
    w2 = e * w1
    idx_ref[0] = jnp.where(lane == 0, i1, jnp.where(lane == 1, i2, 0))
    wt_ref[0] = jnp.where(lane == 0, w1, jnp.where(lane == 1, w2, 0.0))


def _norm_mod_router(h, gain, mods, router_w, router_b):
    b, s, d = h.shape
    ne = router_w.shape[1]
    rw = jnp.zeros((d, LANES), F32).at[:, :ne].set(router_w)
    rb = jnp.full((1, LANES), -1e30, F32).at[0, :ne].set(router_b)
    return pl.pallas_call(
        _norm_mod_router_kernel,
        grid=(b, s // ROW_TILE),
        in_specs=[pl.BlockSpec((1, ROW_TILE, d), lambda i, r: (i, r, 0)),
                  pl.BlockSpec((1, d), lambda i, r: (0, 0)),
                  pl.BlockSpec((1, 1, 2, d), lambda i, r: (i, 1, 0, 0)),
                  pl.BlockSpec((d, LANES), lambda i, r: (0, 0)),
                  pl.BlockSpec((1, LANES), lambda i, r: (0, 0))],
        out_specs=[pl.BlockSpec((1, ROW_TILE, d), lambda i, r: (i, r, 0)),
                   pl.BlockSpec((1, ROW_TILE, LANES), lambda i, r: (i, r, 0)),
                   pl.BlockSpec((1, ROW_TILE, LANES), lambda i, r: (i, r, 0))],
        out_shape=[jax.ShapeDtypeStruct((b, s, d), BF16),
                   jax.ShapeDtypeStruct((b, s, LANES), jnp.int32),
                   jax.ShapeDtypeStruct((b, s, LANES), F32)],
        compiler_params=_cparams(("parallel", "parallel")),
        name="norm_mod_router",
    )(h, gain.reshape(1, d), mods, rw, rb)


def _gelu_exact(x):
    return 0.5 * x * (1.0 + lax.erf(x * (2.0 ** -0.5)))


def _mm_kernel(*refs, nk, mode, norm):
    n_w = 2 if mode == "swiglu" else 1
    n_in = 1 + n_w + (2 if mode == "resid" else 0) + (2 if norm else 0)
    a_ref, w_ref = refs[:2]
    o_ref = refs[n_in]
    if mode == "swiglu":
        w2_ref = refs[2]
    if mode == "resid":
        h_ref, g_ref = refs[2:4]
    if nk > 1:
        acc_ref = refs[n_in + 1]
        if mode == "swiglu":
            acc2_ref = refs[n_in + 2]
    k = pl.program_id(2)
    if norm:
        gain_ref, mod_ref = refs[1 + n_w:3 + n_w]
        an_ref = refs[n_in + 1]

        @pl.when(pl.program_id(1) == 0)
        def _():
            x = a_ref[...]
            y = x * lax.rsqrt(jnp.mean(x * x, axis=-1, keepdims=True) + EPS) * gain_ref[...]
            an_ref[...] = (y * (1.0 + mod_ref[0, 1:2, :]) + mod_ref[0, 0:1, :]).astype(an_ref.dtype)

        a = an_ref[...]
    else:
        a = a_ref[...]
    p = jnp.dot(a, w_ref[...].astype(BF16), preferred_element_type=F32)
    if mode == "swiglu":
        p2 = jnp.dot(a, w2_ref[...].astype(BF16), preferred_element_type=F32)

    if nk > 1:
        @pl.when(k == 0)
        def _():
            acc_ref[...] = p
            if mode == "swiglu":
                acc2_ref[...] = p2

        @pl.when(k > 0)
        def _():
            acc_ref[...] += p
            if mode == "swiglu":
                acc2_ref[...] += p2

    def finish():
        r = acc_ref[...] if nk > 1 else p
        if mode == "plain":
            o_ref[...] = r.astype(o_ref.dtype)
        elif mode == "gelu":
            o_ref[...] = _gelu_exact(r).astype(o_ref.dtype)
        elif mode == "swiglu":
            r2 = acc2_ref[...] if nk > 1 else p2
            o_ref[...] = (_silu(r) * r2).astype(o_ref.dtype)
        else:
            o_ref[...] = h_ref[...] + g_ref[0] * r

    if nk > 1:
        pl.when(k == nk - 1)(finish)
    else:
        finish()


def _mm(a, w, *, mode, tm, tn, tk, out_dtype, n_out=None, w_col0=0, w2_col0=0, h=None, gate=None,
        rows_per_gate=None, norm_gain=None, norm_mods=None):
    m, kdim = a.shape
    n_out = w.shape[1] if n_out is None else n_out
    nk = kdim // tk
    norm = norm_gain is not None
    assert not norm or (nk == 1 and mode != "resid")
    assert m % tm == 0 and n_out % tn == 0 and kdim % tk == 0 and w_col0 % tn == 0 and w2_col0 % tn == 0
    cj, cj2 = w_col0 // tn, w2_col0 // tn
    in_specs = [pl.BlockSpec((tm, tk), lambda i, j, k: (i, k)),
                pl.BlockSpec((tk, tn), lambda i, j, k: (k, j + cj))]
    args = [a, w]
    n_acc = 0 if nk == 1 else (2 if mode == "swiglu" else 1)
    scratch = [pltpu.VMEM((tm, tn), F32)] * n_acc
    if mode == "swiglu":
        in_specs.append(pl.BlockSpec((tk, tn), lambda i, j, k: (k, j + cj2)))
        args.append(w)
    if mode == "resid":
        tiles_per_gate = rows_per_gate // tm
        in_specs += [pl.BlockSpec((tm, tn), lambda i, j, k: (i, j)),
                     pl.BlockSpec((1, 1, tn), lambda i, j, k: (i // tiles_per_gate, 0, j))]
        args += [h, gate]
    if norm:
        tiles_per_mod = rows_per_gate // tm
        in_specs += [pl.BlockSpec((1, kdim), lambda i, j, k: (0, 0)),
                     pl.BlockSpec((1, 2, kdim), lambda i, j, k: (i // tiles_per_mod, 0, 0))]
        args += [norm_gain.reshape(1, kdim), norm_mods]
        scratch = scratch + [pltpu.VMEM((tm, kdim), BF16)]
    return pl.pallas_call(
        functools.partial(_mm_kernel, nk=nk, mode=mode, norm=norm),
        grid=(m // tm, n_out // tn, nk),
        in_specs=in_specs,
        out_specs=pl.BlockSpec((tm, tn), lambda i, j, k: (i, j)),
        out_shape=jax.ShapeDtypeStruct((m, n_out), out_dtype),
        scratch_shapes=scratch,
        compiler_params=_cparams(("parallel", "arbitrary" if norm else "parallel", "arbitrary")),
        name="mm_" + mode,
    )(*args)


def _conv_kernel(x_ref, w_ref, o_ref, xp_ref, *, n_ctx, n_q_blocks, n_k_blocks):
    s = x_ref.shape[1]
    half = GDN_CONV // 2
    w = [w_ref[i:i + 1, :] for i in range(GDN_CONV)]
    cb = pl.program_id(1)
    q_scale = GDN_HEAD_DIM ** -0.5
    pad = CONV_PAD_ROWS
    zeros = jnp.zeros((pad, LANES), F32)
    base = pad
    for lo, hi in ((0, n_ctx), (n_ctx, s)):
        n = hi - lo
        xp_ref[base - pad:base, :] = zeros
        xp_ref[base:base + n, :] = x_ref[0, lo:hi, :].astype(F32)
        xp_ref[base + n:base + n + pad, :] = zeros
        y = xp_ref[base:base + n, :] * w[half]
        for o in range(-half, half + 1):
            if o != 0:
                y = y + xp_ref[base + o:base + o + n, :] * w[o + half]
        base += n + pad
        y = _silu(y)
        fac = lax.rsqrt(jnp.sum(y * y, axis=-1, keepdims=True) + EPS)
        scale = jnp.where(cb < n_q_blocks, fac * q_scale, jnp.where(cb < n_q_blocks + n_k_blocks, fac, 1.0))
        o_ref[0, lo:hi, :] = (y * scale).astype(o_ref.dtype)


def _gdn_conv(proj, conv_w, n_ctx, key_dim):
    b, s, _ = proj.shape
    c = conv_w.shape[1]
    nqb = key_dim // LANES
    return pl.pallas_call(
        functools.partial(_conv_kernel, n_ctx=n_ctx, n_q_blocks=nqb, n_k_blocks=nqb),
        grid=(b, c // LANES),
        in_specs=[pl.BlockSpec((1, s, LANES), lambda i, j: (i, 0, j)),
                  pl.BlockSpec((GDN_CONV, LANES), lambda i, j: (0, j))],
        out_specs=pl.BlockSpec((1, s, LANES), lambda i, j: (i, 0, j)),
        out_shape=jax.ShapeDtypeStruct((b, s, c), BF16),
        scratch_shapes=[pltpu.VMEM((s + 3 * CONV_PAD_ROWS, LANES), F32)],
        compiler_params=_cparams(("parallel", "parallel")),
        name="gdn_conv",
    )(proj, conv_w)


def _gates_kernel(ba_ref, alog_ref, dtb_ref, o_ref, *, n_dir_lanes):
    x = ba_ref[0]
    lc = lax.broadcasted_iota(jnp.int32, (GDN_CHUNK, LANES), 1)
    half = 2 * n_dir_lanes
    beta = 1.0 / (1.0 + jnp.exp(-x))
    z = x + dtb_ref[...]
    softplus = jnp.maximum(z, 0.0) + jnp.log1p(jnp.exp(-jnp.abs(z)))
    g = -jnp.exp(alog_ref[...]) * softplus
    ri = lax.broadcasted_iota(jnp.int32, (GDN_CHUNK, GDN_CHUNK), 0)
    ci = lax.broadcasted_iota(jnp.int32, (GDN_CHUNK, GDN_CHUNK), 1)
    lower = (ri >= ci).astype(F32)
    upper = (ri <= ci).astype(F32)
    for c in range(x.shape[0] // GDN_CHUNK):
        rows = slice(c * GDN_CHUNK, (c + 1) * GDN_CHUNK)
        gc = g[rows]
        fwd = jnp.dot(lower, gc, preferred_element_type=F32, precision=HIGHEST)
        bwd = jnp.dot(upper, gc, preferred_element_type=F32, precision=HIGHEST)
        gcs =jnp.where(lc < half + n_dir_lanes, fwd, bwd)
        o_ref[0, rows, :] = jnp.where(lc < half, beta[rows], gcs)


def _gdn_gates(ba, a_log, dt_bias):
    b, s, _ = ba.shape
    nd = a_log.shape[-1]
    zeros = jnp.zeros((2 * nd,), F32)
    alog = jnp.concatenate([zeros, a_log.reshape(-1)]).reshape(1, LANES)
    dtb = jnp.concatenate([zeros, dt_bias.reshape(-1)]).reshape(1, LANES)
    return pl.pallas_call(
        functools.partial(_gates_kernel, n_dir_lanes=nd),
        grid=(b, s // GDN_STEP),
        in_specs=[pl.BlockSpec((1, GDN_STEP, LANES), lambda i, r: (i, r, 0)),
                  pl.BlockSpec((1, LANES), lambda i, r: (0, 0)),
                  pl.BlockSpec((1, LANES), lambda i, r: (0, 0))],
        out_specs=pl.BlockSpec((1, GDN_STEP, LANES), lambda i, r: (i, r, 0)),
        out_shape=jax.ShapeDtypeStruct((b, s, LANES), F32),
        compiler_params=_cparams(("parallel", "parallel")),
        name="gdn_gates",
    )(ba, alog, dtb)


def _gdn_kernel(q_ref, k_ref, v_ref, col_ref, o_ref, s_ref, *, direction, n_dir_lanes):
    hg = pl.program_id(1)
    step = pl.program_id(2)
    hq = GDN_QK_HEADS_PER_STEP
    nv = 2 * hq
    c = GDN_CHUNK
    dh = GDN_HEAD_DIM
    n_chunks = GDN_STEP // c

    @pl.when(step == 0)
    def _():
        s_ref[...] = jnp.zeros_like(s_ref)

    col = col_ref[0]
    lane0 = direction * n_dir_lanes + hg * nv
    beta_g = pltpu.roll(col, (LANES - lane0) % LANES, 1)
    gcs_g = pltpu.roll(col, (LANES // 2 - lane0) % LANES, 1)
    gcs_r = gcs_g.T

    ri = lax.broadcasted_iota(jnp.int32, (c, c), 0)
    ci = lax.broadcasted_iota(jnp.int32, (c, c), 1)
    if direction == 0:
        incl, strict = ri >= ci, ri > ci
    else:
        incl, strict = ri <= ci, ri < ci
    eye_b = (ri == ci).astype(BF16)
    blockdiag = (ri // 16) == (ci // 16)
    order = range(n_chunks) if direction == 0 else range(n_chunks - 1, -1, -1)
    heads = range(nv)
    order = list(order)
    groups = [order[i:i + GDN_CHUNKS_PER_STAGE] for i in range(0, n_chunks, GDN_CHUNKS_PER_STAGE)]
    for grp in groups:
        rows = {p: slice(p * c, (p + 1) * c) for p in grp}
        pq = [(p, qh) for p in grp for qh in range(hq)]
        pj = [(p, j) for p in grp for j in heads]
        ks = {(p, qh): k_ref[0, rows[p], qh * dh:(qh + 1) * dh] for p, qh in pq}
        qs = {(p, qh): q_ref[0, rows[p], qh * dh:(qh + 1) * dh] for p, qh in pq}
        kq = {x: _dot_nt(jnp.concatenate([ks[x], qs[x]], axis=0), ks[x]) for x in pq}
        kk = {x: kq[x][:c] for x in pq}
        qk = {x: kq[x][c:] for x in pq}
        k_t = {x: ks[x].astype(F32).T for x in pq}
        bcol = {(p, j): beta_g[rows[p], j:j + 1] for p, j in pj}
        gcol = {(p, j): gcs_g[rows[p], j:j + 1] for p, j in pj}
        grow = {(p, j): gcs_r[j:j + 1, rows[p]] for p, j in pj}
        glast = {x: (g[:, c - 1:c] if direction == 0 else g[:, 0:1]) for x, g in grow.items()}
        decay = {x: jnp.exp(jnp.where(incl, gcol[x] - grow[x], -jnp.inf)) for x in pj}
        a = {(p, j): jnp.where(strict, kk[p, j // 2] * bcol[p, j] * decay[p, j], 0.0) for p, j in pj}
        a_df = {x: jnp.where(blockdiag, a[x], 0.0) for x in pj}
        a_o = {x: (a[x] - a_df[x]).astype(BF16) for x in pj}
        a_d = {x: a_df[x].astype(BF16) for x in pj}
        p2 = {x: _dotb(a_d[x], a_d[x]) for x in pj}
        p4 = {x: _dotb(p2[x], p2[x]) for x in pj}
        p8 = {x: _dotb(p4[x], p4[x]) for x in pj}
        t1 = {x: _dotb(eye_b - a_d[x], eye_b + p2[x]) for x in pj}
        t2 = {x: _dotb(t1[x], eye_b + p4[x]) for x in pj}
        t_d = {x: _dotb(t2[x], eye_b + p8[x]) for x in pj}
        nn = {x: _dotb(t_d[x], a_o[x]) for x in pj}
        n2 = {x: _dotb(nn[x], nn[x]) for x in pj}
        rr = {x: _dotb(eye_b - nn[x], eye_b + n2[x]) for x in pj}
        t = {x: _dotb(rr[x], t_d[x]) for x in pj}
        rhs = {(p, j): jnp.concatenate([v_ref[0, rows[p], j * dh:(j + 1) * dh].astype(F32) * bcol[p, j],
                                        ks[p, j // 2].astype(F32) * (bcol[p, j] * jnp.exp(gcol[p, j]))],
                                       axis=1).astype(BF16)
               for p, j in pj}
        uw = {x: _dot(t[x], rhs[x]) for x in pj}
        wq_lhs = {(p, j): jnp.concatenate([uw[p, j][:, dh:], qs[p, j // 2].astype(F32) * jnp.exp(gcol[p, j])],
                                          axis=0).astype(BF16)
                  for p, j in pj}
        lhs2 = {(p, j): jnp.concatenate([jnp.where(incl, qk[p, j // 2] * decay[p, j], 0.0),
                                         k_t[p, j // 2] * jnp.exp(glast[p, j] - grow[p, j])], axis=0).astype(BF16)
                for p, j in pj}
        for p in grp:
            state = [s_ref[j] for j in heads]
            wq = [_dot(wq_lhs[p, j], state[j]) for j in heads]
            v_new = [uw[p, j][:, :dh] - wq[j][:c] for j in heads]
            r2 = [_dot(lhs2[p, j], v_new[j]) for j in heads]
            for j in heads:
                o_ref[0, rows[p], j * dh:(j + 1) * dh] = (wq[j][c:] + r2[j][:c]).astype(o_ref.dtype)
                s_ref[j] = state[j] * jnp.exp(glast[p, j]) + r2[j][c:]


def _gdn_scan(qkv, col, direction, n_ctx, key_dim, n_v_heads):
    b, s, _ = qkv.shape
    hq = GDN_QK_HEADS_PER_STEP
    nv = 2 * hq
    n_steps = s // GDN_STEP
    n_ctx_steps = n_ctx // GDN_STEP
    assert n_ctx_steps == 1
    qw = hq * GDN_HEAD_DIM
    vw = nv * GDN_HEAD_DIM
    n_groups = key_dim // qw
    k_blk0 = key_dim // qw
    v_blk0 = 2 * key_dim // vw

    def blk(st):
        if direction == 0:
            return st
        return jnp.where(st == 0, 0, n_steps - st)

    def oblk(st):
        return blk(jnp.maximum(st, 1)) - 1

    return pl.pallas_call(
        functools.partial(_gdn_kernel, direction=direction, n_dir_lanes=n_v_heads),
        grid=(b, n_groups, n_steps),
        in_specs=[pl.BlockSpec((1, GDN_STEP, qw), lambda i, g, st: (i, blk(st), g)),
                  pl.BlockSpec((1, GDN_STEP, qw), lambda i, g, st: (i, blk(st), k_blk0 + g)),
                  pl.BlockSpec((1, GDN_STEP, vw), lambda i, g, st: (i, blk(st), v_blk0 + g)),
                  pl.BlockSpec((1, GDN_STEP, LANES), lambda i, g, st: (i, blk(st), 0))],
        out_specs=pl.BlockSpec((1, GDN_STEP, vw), lambda i, g, st: (i, oblk(st), g)),
        out_shape=jax.ShapeDtypeStruct((b, s - n_ctx, n_v_heads * GDN_HEAD_DIM), BF16),
        scratch_shapes=[pltpu.VMEM((nv, GDN_HEAD_DIM, GDN_HEAD_DIM), F32)],
        compiler_params=_cparams(("parallel", "parallel", "arbitrary")),
        name="gdn_scan_d%d" % direction,
    )(qkv, qkv, qkv, col)


def _gdn_out_kernel(o0_ref, o1_ref, z_ref, g_ref, y_ref):
    dh = GDN_HEAD_DIM
    for hd in range(o0_ref.shape[2] // dh):
        sl = slice(hd * dh, (hd + 1) * dh)
        o = o0_ref[0, :, sl].astype(F32) + o1_ref[0, :, sl].astype(F32)
        y = o * lax.rsqrt(jnp.mean(o * o, axis=-1, keepdims=True) + EPS) * g_ref[...]
        y_ref[0, :, sl] = (y * _silu(z_ref[0, :, sl].astype(F32))).astype(y_ref.dtype)


def _gdn_out_norm(o0, o1, proj, o_norm, n_ctx, z_col0):
    b, l, w = o0.shape
    ctx_tiles = n_ctx // ROW_TILE
    zb = z_col0 // w
    return pl.pallas_call(
        _gdn_out_kernel,
        grid=(b, l // ROW_TILE),
        in_specs=[pl.BlockSpec((1, ROW_TILE, w), lambda i, r: (i, r, 0)),
                  pl.BlockSpec((1, ROW_TILE, w), lambda i, r: (i, r, 0)),
                  pl.BlockSpec((1, ROW_TILE, w), lambda i, r: (i, r + ctx_tiles, zb)),
                  pl.BlockSpec((1, GDN_HEAD_DIM), lambda i, r: (0, 0))],
        out_specs=pl.BlockSpec((1, ROW_TILE, w), lambda i, r: (i, r, 0)),
        out_shape=jax.ShapeDtypeStruct((b, l, w), BF16),
        compiler_params=_cparams(("parallel", "parallel")),
        name="gdn_out_norm",
    )(o0, o1, proj, o_norm.reshape(1, -1))


def _cm_kernel(gu_ref, gv_ref, lg_ref, lb_ref, ws_ref, bs_ref, o_ref):
    gd = gu_ref.shape[1] // CM_GROUPS
    for ch in range(gu_ref.shape[0] // CM_CHUNK):
        rows = slice(ch * CM_CHUNK, (ch + 1) * CM_CHUNK)
        gv = gv_ref[rows, :].astype(F32)
        mu = jnp.mean(gv, axis=-1, keepdims=True)
        xc = gv - mu
        var = jnp.mean(xc * xc, axis=-1, keepdims=True)
        gvn = xc * lax.rsqrt(var + EPS) * lg_ref[...] + lb_ref[...]
        for g in range(CM_GROUPS):
            sl = slice(g * gd, (g + 1) * gd)
            mixed = _dot(ws_ref[g], gvn[:, sl]) + bs_ref[:, g:g + 1]
            o_ref[rows, sl] = (gu_ref[rows, sl].astype(F32) * mixed).astype(o_ref.dtype)


def _cm_spatial(guv, ln_g, ln_b, w_s, b_s):
    t, w2 = guv.shape
    w = w2 // 2
    rt = 2 * CM_CHUNK
    return pl.pallas_call(
        _cm_kernel,
        grid=(t // rt,),
        in_specs=[pl.BlockSpec((rt, w), lambda i: (i, 0)),
                  pl.BlockSpec((rt, w), lambda i: (i, 1)),
                  pl.BlockSpec((1, w), lambda i: (0, 0)),
                  pl.BlockSpec((1, w), lambda i: (0, 0)),
                  pl.BlockSpec((CM_GROUPS, CM_CHUNK, CM_CHUNK), lambda i: (0, 0, 0)),
                  pl.BlockSpec((CM_CHUNK, CM_GROUPS), lambda i: (0, 0))],
        out_specs=pl.BlockSpec((rt, w), lambda i: (i, 0)),
        out_shape=jax.ShapeDtypeStruct((t, w), BF16),
        compiler_params=_cparams(("parallel",)),
        name="cm_spatial",
    )(guv, guv, ln_g.reshape(1, w), ln_b.reshape(1, w), w_s, jnp.transpose(b_s))


def _moe_plan(idx2):
    t = idx2.shape[0]
    a = t * TOP_K
    e = idx2.reshape(a)
    onehot = (e[:, None] == jnp.arange(N_EXPERTS, dtype=jnp.int32)[None, :]).astype(jnp.int32)
    rank = jnp.sum((jnp.cumsum(onehot, axis=0) - onehot) * onehot, axis=1)
    counts = jnp.sum(onehot, axis=0)
    padded = ((counts + MOE_TM - 1) // MOE_TM) * MOE_TM
    ends = jnp.cumsum(padded)
    starts = ends - padded
    pos = starts[e] + rank
    a_pad = a + N_EXPERTS * MOE_TM
    n_tiles = a_pad // MOE_TM
    tile_start = jnp.arange(n_tiles, dtype=jnp.int32) * MOE_TM
    tile_expert = jnp.minimum(jnp.searchsorted(ends, tile_start, side="right"), N_EXPERTS - 1).astype(jnp.int32)
    tile_rows = jnp.clip((starts + counts)[tile_expert] - tile_start, 0, MOE_TM).astype(jnp.int32)
    tile_expert = jnp.where(tile_rows > 0, tile_expert, tile_expert[ends[-1] // MOE_TM - 1])
    return pos.reshape(t, TOP_K), a_pad, tile_expert, tile_rows


def _scatter_kernel(pos_ref, src_ref, init_ref, dst_ref, sem):
    del init_ref
    tt = src_ref.shape[0]

    def issue(r, carry):
        for k in range(TOP_K):
            pltpu.make_async_copy(src_ref.at[r], dst_ref.at[pos_ref[0, 0, k * tt + r]], sem).start()
        return carry

    lax.fori_loop(0, tt, issue, 0, unroll=DMA_UNROLL)

    def drain(r, carry):
        pltpu.make_async_copy(src_ref.at[0], dst_ref.at[0], sem).wait()
        return carry

    lax.fori_loop(0, TOP_K * tt, drain, 0, unroll=DMA_UNROLL)


def _scatter_rows(src3, pos_tiles, n_rows):
    nb, _, n_pos = pos_tiles.shape
    tt = n_pos // TOP_K
    out_shape = jax.ShapeDtypeStruct((n_rows,) + src3.shape[1:], src3.dtype)
    return pl.pallas_call(
        _scatter_kernel,
        grid=(nb,),
        in_specs=[pl.BlockSpec((1, 1, n_pos), lambda i: (i, 0, 0), memory_space=pltpu.SMEM),
                  pl.BlockSpec((tt,) + src3.shape[1:], lambda i: (i, 0, 0)),
                  pl.BlockSpec(memory_space=pl.ANY)],
        out_specs=pl.BlockSpec(memory_space=pl.ANY),
        out_shape=out_shape,
        input_output_aliases={2: 0},
        scratch_shapes=[pltpu.SemaphoreType.DMA(())],
        compiler_params=_cparams(("arbitrary",)),
        name="scatter_rows",
    )(pos_tiles, src3, jnp.zeros(out_shape.shape, out_shape.dtype))


def _moe_gu_kernel(te_ref, tr_ref, x2_ref, wg_ref, wu_ref, o_ref, wb_ref):
    rows = tr_ref[pl.program_id(0)]
    tm = o_ref.shape[0]

    def swiglu(x, wg, wu):
        g = jnp.dot(x, wg, preferred_element_type=F32)
        u = jnp.dot(x, wu, preferred_element_type=F32)
        return (_silu(g) * u).astype(o_ref.dtype)

    @pl.when(rows == tm)
    def _():
        o_ref[...] = swiglu(x2_ref[...], wg_ref[0].astype(BF16), wu_ref[0].astype(BF16))

    @pl.when((rows > 0) & (rows < tm))
    def _():
        wb_ref[0] = wg_ref[0].astype(BF16)
        wb_ref[1] = wu_ref[0].astype(BF16)
        for sb in range(tm // MOE_SUB):
            sl = slice(sb * MOE_SUB, (sb + 1) * MOE_SUB)

            @pl.when(sb * MOE_SUB < rows)
            def _():
                o_ref[sl, :] = swiglu(x2_ref[sl, :], wb_ref[0], wb_ref[1])

            @pl.when(sb * MOE_SUB >= rows)
            def _():
                o_ref[sl, :] = jnp.zeros((MOE_SUB, o_ref.shape[1]), o_ref.dtype)

    @pl.when(rows == 0)
    def _():
        o_ref[...] = jnp.zeros_like(o_ref)


def _moe_gu(xs, w_gu, tile_expert, tile_rows, tn=512):
    a_pad, d = xs.shape
    f = w_gu.shape[2] // 2
    nj = f // tn

    def wj(i, j, tr):
        return jnp.where(tr[i] > 0, j, nj - 1)

    return pl.pallas_call(
        _moe_gu_kernel,
        grid_spec=pltpu.PrefetchScalarGridSpec(
            num_scalar_prefetch=2,
            grid=(a_pad // MOE_TM, nj),
            in_specs=[pl.BlockSpec((MOE_TM, d), lambda i, j, te, tr: (i, 0)),
                      pl.BlockSpec((1, d, tn), lambda i, j, te, tr: (te[i], 0, wj(i, j, tr))),
                      pl.BlockSpec((1, d, tn), lambda i, j, te, tr: (te[i], 0, wj(i, j, tr) + nj))],
            out_specs=pl.BlockSpec((MOE_TM, tn), lambda i, j, te, tr: (i, j)),
            scratch_shapes=[pltpu.VMEM((2, d, tn), BF16)]),
        out_shape=jax.ShapeDtypeStruct((a_pad, f), BF16),
        compiler_params=_cparams(("parallel", "parallel")),
        name="moe_gu",
    )(tile_expert, tile_rows, xs, w_gu, w_gu)


def _moe_down_kernel(te_ref, tr_ref, a_ref, w_ref, o_ref, acc_ref, wb_ref, *, nk):
    k = pl.program_id(2)
    rows = tr_ref[pl.program_id(0)]
    tm = acc_ref.shape[0]

    @pl.when(rows == tm)
    def _():
        p = jnp.dot(a_ref[...], w_ref[0].astype(BF16), preferred_element_type=F32)

        @pl.when(k == 0)
        def _():
            acc_ref[...] = p

        @pl.when(k > 0)
        def _():
            acc_ref[...] += p

    @pl.when((rows > 0) & (rows < tm))
    def _():
        @pl.when(k == 0)
        def _():
            acc_ref[...] = jnp.zeros_like(acc_ref)

        wb_ref[...] = w_ref[0].astype(BF16)
        for sb in range(tm // MOE_SUB):
            sl = slice(sb * MOE_SUB, (sb + 1) * MOE_SUB)

            @pl.when(sb * MOE_SUB < rows)
            def _():
                acc_ref[sl, :] += jnp.dot(a_ref[sl, :], wb_ref[...], preferred_element_type=F32)

    @pl.when((rows > 0) & (k == nk - 1))
    def _():
        o_ref[...] = acc_ref[...].astype(o_ref.dtype)

    @pl.when((rows == 0) & (k == nk - 1))
    def _():
        o_ref[...] = jnp.zeros_like(o_ref)


def _moe_down(hmid, w_down, tile_expert, tile_rows, tn=1024, tk=1792):
    a_pad, f = hmid.shape
    d = w_down.shape[2]
    nk = f // tk
    nj = d // tn

    def live(i, tr, idx, last):
        return jnp.where(tr[i] > 0, idx, last)

    return pl.pallas_call(
        functools.partial(_moe_down_kernel, nk=nk),
        grid_spec=pltpu.PrefetchScalarGridSpec(
            num_scalar_prefetch=2,
            grid=(a_pad // MOE_TM, nj, nk),
            in_specs=[pl.BlockSpec((MOE_TM, tk), lambda i, j, k, te, tr: (i, live(i, tr, k, nk - 1))),
                      pl.BlockSpec((1, tk, tn),
                                   lambda i, j, k, te, tr: (te[i], live(i, tr, k, nk - 1), live(i, tr, j, nj - 1)))],
            out_specs=pl.BlockSpec((MOE_TM, tn), lambda i, j, k, te, tr: (i, j)),
            scratch_shapes=[pltpu.VMEM((MOE_TM, tn), F32), pltpu.VMEM((tk, tn), BF16)]),
        out_shape=jax.ShapeDtypeStruct((a_pad, d), BF16),
        compiler_params=_cparams(("parallel", "parallel", "arbitrary")),
        name="moe_down",
    )(tile_expert, tile_rows, hmid, w_down)


def _combine_kernel(pos_ref, posn_ref, y_ref, wt_ref, h_ref, gt_ref, gn_ref, o_ref, buf_ref, sem):
    i = pl.program_id(0)
    slot = i % 2
    n = buf_ref.shape[1]

    def gather(idx_ref, sl):
        def issue(r, carry):
            pltpu.make_async_copy(y_ref.at[idx_ref[0, 0, r]], buf_ref.at[sl, r], sem.at[sl]).start()
            return carry

        lax.fori_loop(0, n, issue, 0, unroll=DMA_UNROLL)

    @pl.when(i == 0)
    def _():
        gather(pos_ref, slot)

    @pl.when(i + 1 < pl.num_programs(0))
    def _():
        gather(posn_ref, 1 - slot)

    def drain(r, carry):
        pltpu.make_async_copy(y_ref.at[0], buf_ref.at[slot, 0], sem.at[slot]).wait()
        return carry

    lax.fori_loop(0, n, drain, 0, unroll=DMA_UNROLL)
    tt = o_ref.shape[0]
    w0 = wt_ref[:, 0:1]
    w1 = wt_ref[:, 1:2]
    ss = jnp.zeros((tt, 1), F32)
    for s in range(buf_ref.shape[2]):
        sl = slice(s * LANES, (s + 1) * LANES)
        y = (w0 * buf_ref[slot, pl.ds(0, tt), s, :].astype(F32)
             + w1 * buf_ref[slot, pl.ds(tt, tt), s, :].astype(F32))
        hn = h_ref[:, sl] + gt_ref[0, :, sl] * y
        ss = ss + jnp.sum(hn * hn, axis=-1, keepdims=True)
        o_ref[:, sl] = hn
    o_ref[...] = o_ref[...] * lax.rsqrt(ss * (1.0 / o_ref.shape[1]) + EPS) * gn_ref[...]


def _moe_combine_final(y3, pos_tiles, wt, h, gate, gain, tokens_per_batch, tt=256):
    t, d = h.shape
    sd = d // LANES
    tiles_per_batch = tokens_per_batch // tt
    n_steps = t // tt
    return pl.pallas_call(
        _combine_kernel,
        grid=(n_steps,),
        in_specs=[pl.BlockSpec((1, 1, TOP_K * tt), lambda i: (i, 0, 0), memory_space=pltpu.SMEM),
                  pl.BlockSpec((1, 1, TOP_K * tt), lambda i: (jnp.minimum(i + 1, n_steps - 1), 0, 0),
                               memory_space=pltpu.SMEM),
                  pl.BlockSpec(memory_space=pl.ANY),
                  pl.BlockSpec((tt, LANES), lambda i: (i, 0)),
                  pl.BlockSpec((tt, d), lambda i: (i, 0)),
                  pl.BlockSpec((1, 1, d), lambda i: (i // tiles_per_batch, 0, 0)),
                  pl.BlockSpec((1, d), lambda i: (0, 0))],
        out_specs=pl.BlockSpec((tt, d), lambda i: (i, 0)),
        out_shape=jax.ShapeDtypeStruct((t, d), F32),
        scratch_shapes=[pltpu.VMEM((2, TOP_K * tt, sd, LANES), y3.dtype), pltpu.SemaphoreType.DMA((2,))],
        compiler_params=_cparams(("arbitrary",)),
        name="moe_combine",
    )(pos_tiles, pos_tiles, y3, wt, h, gate, gain.reshape(1, d))


def kernel(x, c, ctx, c_ctx, ada_w, ada_b, norm_mix, norm_ffn, norm_final, gdn_w_in, gdn_conv, gdn_a_log,
           gdn_dt_bias, gdn_o_norm, gdn_w_out, cm_w_in, cm_ln_g, cm_ln_b, cm_w_s, cm_b_s, cm_w_out, ffn_w_gu,
           ffn_w_down, moe_router, moe_router_b, moe_w_gu, moe_w_down):
    b, l, d = x.shape
    n_ctx = ctx.shape[1]
    s = n_ctx + l
    t = b * l
    n_v_heads = gdn_a_log.shape[-1]
    val_dim = n_v_heads * GDN_HEAD_DIM
    key_dim = (gdn_conv.shape[-1] - val_dim) // 2
    conv_dim = 2 * key_dim + val_dim
    ffn_dim = ffn_w_down.shape[1]

    cond8 = jnp.zeros((8, d), F32).at[:b].set(c).at[b].set(c_ctx)
    mods = _ada(cond8, ada_w, ada_b).reshape(ada_w.shape[0], 8, 6, d)

    def seg_mods(layer, first):
        lat = mods[layer, :b, first:first + 2]
        cx = jnp.broadcast_to(mods[layer, b, first:first + 2][None], lat.shape)
        return jnp.stack([cx, lat], axis=1)

    u = _norm_mod(x, norm_mix[0], seg_mods(0, 0), ctx).reshape(b * s, d)
    w_in = gdn_w_in[0]
    n_qkvz = conv_dim + val_dim
    tm_cat = _tile(b * s, MM_TM)
    tm = _tile(l, MM_TM)
    proj = _mm(u, w_in, mode="plain", tm=tm_cat, tn=1024, tk=d, out_dtype=BF16, n_out=n_qkvz).reshape(b, s, n_qkvz)
    ba = _mm(u, w_in, mode="plain", tm=tm_cat, tn=LANES, tk=d, out_dtype=F32, n_out=LANES, w_col0=n_qkvz)
    qkv = _gdn_conv(proj, gdn_conv[0], n_ctx, key_dim)
    col = _gdn_gates(ba.reshape(b, s, LANES), gdn_a_log[0], gdn_dt_bias[0])
    o_fwd = _gdn_scan(qkv, col, 0, n_ctx, key_dim, n_v_heads)
    o_bwd = _gdn_scan(qkv, col, 1, n_ctx, key_dim, n_v_heads)
    y = _gdn_out_norm(o_fwd, o_bwd, proj, gdn_o_norm[0], n_ctx, conv_dim).reshape(t, val_dim)
    gate = lambda layer, idx: mods[layer, :b, idx].reshape(b, 1, d)
    h = _mm(y, gdn_w_out[0], mode="resid", tm=tm,tn=512, tk=val_dim, out_dtype=F32, h=x.reshape(t, d),
            gate=gate(0, 2), rows_per_gate=l)

    mid = _mm(h, ffn_w_gu[0], mode="swiglu", tm=tm, tn=512, tk=d, out_dtype=BF16, n_out=ffn_dim, w2_col0=ffn_dim,
              rows_per_gate=l, norm_gain=norm_ffn[0], norm_mods=mods[0, :b, 3:5])
    h = _mm(mid, ffn_w_down[0], mode="resid", tm=tm,tn=256, tk=ffn_dim, out_dtype=F32, h=h, gate=gate(0, 5),
            rows_per_gate=l)

    guv = _mm(h, cm_w_in[0], mode="gelu", tm=tm, tn=512, tk=d, out_dtype=BF16,
              rows_per_gate=l, norm_gain=norm_mix[1], norm_mods=mods[1, :b, 0:2])
    cmix = _cm_spatial(guv, cm_ln_g[0], cm_ln_b[0], cm_w_s[0], cm_b_s[0])
    h = _mm(cmix, cm_w_out[0], mode="resid", tm=tm,tn=512, tk=cmix.shape[1], out_dtype=F32, h=h, gate=gate(1, 2),
            rows_per_gate=l)

    tf, idx, wt = _norm_mod_router(h.reshape(b, l, d), norm_ffn[1], seg_mods(1, 3), moe_router[0], moe_router_b[0])
    pos, a_pad, tile_expert, tile_rows = _moe_plan(idx.reshape(t, LANES)[:, :TOP_K])
    tt = ROW_TILE
    pos_tiles = jnp.transpose(pos.reshape(t // tt, tt, TOP_K), (0, 2, 1)).reshape(t // tt, 1, TOP_K * tt)
    sd = d // LANES
    xs = _scatter_rows(tf.reshape(t, sd, LANES), pos_tiles, a_pad).reshape(a_pad, d)
    hmid = _moe_gu(xs, moe_w_gu[0], tile_expert, tile_rows)
    ys3 = _moe_down(hmid, moe_w_down[0], tile_expert, tile_rows).reshape(a_pad, sd, LANES)
    out = _moe_combine_final(ys3, pos_tiles, wt.reshape(t, LANES), h, gate(1, 5), norm_final, l, tt)
    return out.reshape(b, l, d)
```
